```python
import math
import jax, jax.numpy as jnp
from jax import lax
import numpy as np

D_MODEL = 1024
BATCH = 32
SEQ = 2048
DEPTH = 2
DEC_BATCH = 32
DEC_SEQ = 16
PAST_LEN = 2048

CHUNK = 64
QBLK = 128
ROPE_THETA = 500000.0
NORM_EPS = 1e-6
A_HEADS = 4
A_QK_DIM = 64
A_V_DIM = 2 * A_QK_DIM
A_ROT = A_QK_DIM // 4
A_WIDTH = A_HEADS * A_V_DIM
B_HEADS = 8
B_HD = 64
B_WIDTH = B_HEADS * B_HD
B_LEFT_CHUNKS = 8
B_REL_CLIP = 128
C_HEADS = 8
C_NOPE = 64
C_ROPE = 32
C_V = 64
C_Q_LORA = 768
C_KV_LORA = 256
C_WIDTH = C_HEADS * C_V
D_HEADS = 8
D_HD = 64
D_WIDTH = D_HEADS * D_HD
EVEN_SPLITS = (A_HEADS * 2 * A_QK_DIM, A_HEADS * 2 * A_QK_DIM, A_WIDTH, B_WIDTH, B_WIDTH, B_WIDTH, A_WIDTH + B_WIDTH)
ODD_SPLITS = (C_Q_LORA, C_KV_LORA, C_ROPE, D_WIDTH, D_WIDTH, D_WIDTH, C_WIDTH + D_WIDTH)
EVEN_IN = 4 * A_HEADS * A_QK_DIM + A_WIDTH + 3 * B_WIDTH + A_WIDTH + B_WIDTH
ODD_IN = C_Q_LORA + C_KV_LORA + C_ROPE + 3 * D_WIDTH + C_WIDTH + D_WIDTH
MIX_EVEN = A_WIDTH + B_WIDTH
MIX_ODD = C_WIDTH + D_WIDTH

kernel_name = 'hybrid_chunk_streaming_encoder_step'


def rmsnorm(x, g):
    xf = x.astype(jnp.float32)
    y = xf * lax.rsqrt(jnp.mean(xf * xf, axis=-1, keepdims=True) + NORM_EPS)
    return (y * g.astype(jnp.float32)).astype(x.dtype)


def split_cols(z, sizes):
    idx = np.cumsum(np.array(sizes))[:-1].tolist()
    return jnp.split(z, idx, axis=-1)


def rope(x, pos, rot_dim):
    half = rot_dim // 2
    inv_freq = ROPE_THETA ** (-jnp.arange(half, dtype=jnp.float32) / half)
    ang = pos.astype(jnp.float32)[:, None] * inv_freq[None, :]
    bshape = (pos.shape[0],) + (1,) * (x.ndim - 3) + (half,)
    cos = jnp.cos(ang).reshape(bshape).astype(x.dtype)
    sin = jnp.sin(ang).reshape(bshape).astype(x.dtype)
    x1, x2, rest = x[..., :half], x[..., half:rot_dim], x[..., rot_dim:]
    return jnp.concatenate([x1 * cos - x2 * sin, x2 * cos + x1 * sin, rest], axis=-1)


def chunk_mask(q_pos, k_pos):
    return (k_pos[None, :] // CHUNK) <= (q_pos[:, None] // CHUNK)


def sweep_queries(fn, *q_args):
    t = q_args[0].shape[1]
    if t <= QBLK or t % QBLK != 0:
        return fn(*q_args)
    nb = t // QBLK
    blocked = tuple(a.reshape((a.shape[0], nb, QBLK) + a.shape[2:]).swapaxes(0, 1) for a in q_args)
    out = lax.map(lambda args: fn(*args), blocked)
    out = out.swapaxes(0, 1)
    return out.reshape((out.shape[0], t) + out.shape[3:])


def diff_attention(q, k, v, q_pos, k_pos, lam, lam_init, g_sub):
    scale = A_QK_DIM ** -0.5

    def block(qb, pb):
        s = jnp.einsum('bthmd,bshmd->bhmts', qb, k).astype(jnp.float32) * scale
        s = jnp.where(chunk_mask(pb[0], k_pos)[None, None, None], s, -jnp.inf)
        p = jax.nn.softmax(s, axis=-1)
        w = p[:, :, 0] - lam * p[:, :, 1]
        return jnp.einsum('bhts,bshd->bthd', w.astype(v.dtype), v)

    o = sweep_queries(block, q, q_pos[None])
    o = rmsnorm(o, g_sub) * (1.0 - lam_init)
    return o.reshape(o.shape[0], o.shape[1], A_WIDTH)


def band_attend(q, k, v, q_pos, k_pos, rel_bias, valid):
    s = jnp.einsum('bthd,bshd->bhts', q, k).astype(jnp.float32) * (B_HD ** -0.5)
    rel = jnp.clip(q_pos[:, None] - k_pos[None, :], -B_REL_CLIP, B_REL_CLIP) + B_REL_CLIP
    s = s + rel_bias[:, rel].astype(jnp.float32)[None]
    s = jnp.where(valid[None, None, None, :], s, -jnp.inf)
    p = jax.nn.softmax(s, axis=-1)
    return jnp.einsum('bhts,bshd->bthd', p.astype(v.dtype), v)


def chunk_band_prompt(q, k, v, rel_bias):
    bn, s, h, d = q.shape
    n_chunks = s // CHUNK
    pad = B_LEFT_CHUNKS * CHUNK
    band = pad + CHUNK
    kp = jnp.pad(k, ((0, 0), (pad, 0), (0, 0), (0, 0)))
    vp = jnp.pad(v, ((0, 0), (pad, 0), (0, 0), (0, 0)))
    qc = q.reshape(bn, n_chunks, CHUNK, h, d).swapaxes(0, 1)

    def one(args):
        c, qb = args
        start = c * CHUNK
        kb = lax.dynamic_slice_in_dim(kp, start, band, axis=1)
        vb = lax.dynamic_slice_in_dim(vp, start, band, axis=1)
        q_pos = start + jnp.arange(CHUNK)
        k_pos = start - pad + jnp.arange(band)
        return band_attend(qb, kb, vb, q_pos, k_pos, rel_bias, k_pos >= 0)

    o = lax.map(one, (jnp.arange(n_chunks), qc))
    return o.swapaxes(0, 1).reshape(bn, s, h, d)


def mla_attention(q_nope, q_rope, ckv, krope, q_pos, k_pos, w_uk, w_uv):
    q_lat = jnp.einsum('bthd,chd->bthc', q_nope, w_uk)
    scale = (C_NOPE + C_ROPE) ** -0.5

    def block(ql, qr, pb):
        s = (jnp.einsum('bthc,bsc->bhts', ql, ckv) + jnp.einsum('bthr,bsr->bhts', qr, krope)).astype(jnp.float32) * scale
        s = jnp.where(chunk_mask(pb[0], k_pos)[None, None], s, -jnp.inf)
        p = jax.nn.softmax(s, axis=-1)
        return jnp.einsum('bhts,bsc->bthc', p.astype(ckv.dtype), ckv)

    o_lat = sweep_queries(block, q_lat, q_rope, q_pos[None])
    o = jnp.einsum('bthc,chd->bthd', o_lat, w_uv)
    return o.reshape(o.shape[0], o.shape[1], C_WIDTH)


def stick_breaking(q, k, v, q_pos, k_pos):
    scale = D_HD ** -0.5

    def block(qb, pb):
        z = jnp.einsum('bthd,bshd->bhts', qb, k).astype(jnp.float32) * scale
        m = (k_pos[None, :] < pb[0][:, None])[None, None]
        log_beta = jax.nn.log_sigmoid(z)
        log_1m = jnp.where(m, jax.nn.log_sigmoid(-z), 0.0)
        later = lax.cumsum(log_1m, axis=3, reverse=True) - log_1m
        a = jnp.where(m, jnp.exp(log_beta + later), 0.0)
        return jnp.einsum('bhts,bshd->bthd', a.astype(v.dtype), v)

    o = sweep_queries(block, q, q_pos[None])
    return o.reshape(o.shape[0], o.shape[1], D_WIDTH)


def layer_even(h, pos, cache, layer, g_pre, w_in, lam_q1, lam_k1, lam_q2, lam_k2, g_sub, rel_bias, w_out, g_post):
    bn, t, _ = h.shape
    u = rmsnorm(h, g_pre)
    aq, ak, av, bq, bk, bv, gate = split_cols(u @ w_in, EVEN_SPLITS)
    aq = rope(aq.reshape(bn, t, A_HEADS, 2, A_QK_DIM), pos, A_ROT)
    ak = rope(ak.reshape(bn, t, A_HEADS, 2, A_QK_DIM), pos, A_ROT)
    av = av.reshape(bn, t, A_HEADS, A_V_DIM)
    bq = bq.reshape(bn, t, B_HEADS, B_HD)
    bk = bk.reshape(bn, t, B_HEADS, B_HD)
    bv = bv.reshape(bn, t, B_HEADS, B_HD)
    lam_init = 0.8 - 0.6 * math.exp(-0.3 * layer)
    lam = (jnp.exp(jnp.sum(lam_q1.astype(jnp.float32) * lam_k1.astype(jnp.float32)))
           - jnp.exp(jnp.sum(lam_q2.astype(jnp.float32) * lam_k2.astype(jnp.float32))) + lam_init)
    if cache is None:
        ak_all, av_all, k_pos_a = ak, av, pos
        ob = chunk_band_prompt(bq, bk, bv, rel_bias)
        b_rows = min(B_LEFT_CHUNKS * CHUNK, t)
        b_k_new, b_v_new = bk[:, t - b_rows:], bv[:, t - b_rows:]
    else:
        ca_k, ca_v, cb_k, cb_v = cache
        past = ca_k.shape[1]
        ak_all = jnp.concatenate([ca_k, ak], axis=1)
        av_all = jnp.concatenate([ca_v, av], axis=1)
        k_pos_a = jnp.arange(past + t)
        band_len = cb_k.shape[1]
        k_pos_b = past - band_len + jnp.arange(band_len + t)
        ob = band_attend(bq, jnp.concatenate([cb_k, bk], axis=1), jnp.concatenate([cb_v, bv], axis=1),
                         pos, k_pos_b, rel_bias, k_pos_b >= 0)
        b_k_new, b_v_new = bk, bv
    oa = diff_attention(aq, ak_all, av_all, pos, k_pos_a, lam, lam_init, g_sub)
    o = jnp.concatenate([oa, ob.reshape(bn, t, B_WIDTH)], axis=-1) * jax.nn.silu(gate)
    h = h + rmsnorm(o @ w_out, g_post)
    return h, (ak, av, b_k_new, b_v_new)


def layer_odd(h, pos, cache, g_pre, w_in, g_cq, w_uq, g_ckv, w_uk, w_uv, w_out, g_post):
    bn, t, _ = h.shape
    u = rmsnorm(h, g_pre)
    cq, ckv, kr, dq, dk, dv, gate = split_cols(u @ w_in, ODD_SPLITS)
    qc = (rmsnorm(cq, g_cq) @ w_uq).reshape(bn, t, C_HEADS, C_NOPE + C_ROPE)
    q_nope = qc[..., :C_NOPE]
    q_rope = rope(qc[..., C_NOPE:], pos, C_ROPE)
    ckv = rmsnorm(ckv, g_ckv)
    kr = rope(kr, pos, C_ROPE)
    dq = dq.reshape(bn, t, D_HEADS, D_HD)
    dk = dk.reshape(bn, t, D_HEADS, D_HD)
    dv = dv.reshape(bn, t, D_HEADS, D_HD)
    if cache is None:
        lat_all, kr_all, dk_all, dv_all, k_pos = ckv, kr, dk, dv, pos
    else:
        c_lat, c_kr, c_dk, c_dv = cache
        lat_all = jnp.concatenate([c_lat, ckv], axis=1)
        kr_all = jnp.concatenate([c_kr, kr], axis=1)
        dk_all = jnp.concatenate([c_dk, dk], axis=1)
        dv_all = jnp.concatenate([c_dv, dv], axis=1)
        k_pos = jnp.arange(c_lat.shape[1] + t)
    oc = mla_attention(q_nope, q_rope, lat_all, kr_all, pos, k_pos, w_uk, w_uv)
    od = stick_breaking(dq, dk_all, dv_all, pos, k_pos)
    o = jnp.concatenate([oc, od], axis=-1) * jax.nn.silu(gate)
    h = h + rmsnorm(o @ w_out, g_post)
    return h, (ckv, kr, dk, dv)


def setup_inputs(seed: int = 0) -> dict:
    key = jax.random.key(seed)
    ks = jax.random.split(key, 32)

    def nrm(i, shape, scale=1.0):
        return jax.random.normal(ks[i], shape, jnp.float32) * scale

    def gain(i, n):
        return 1.0 + 0.01 * jax.random.normal(ks[i], (n,), jnp.float32)

    b_rows = min(B_LEFT_CHUNKS * CHUNK, PAST_LEN)
    return {
        'x_prompt': nrm(0, (BATCH, SEQ, D_MODEL)),
        'x_sample': nrm(1, (DEC_BATCH, DEC_SEQ, D_MODEL)),
        'cache_a_k': nrm(2, (DEC_BATCH, PAST_LEN, A_HEADS, 2, A_QK_DIM)),
        'cache_a_v': nrm(3, (DEC_BATCH, PAST_LEN, A_HEADS, A_V_DIM)),
        'cache_b_k': nrm(4, (DEC_BATCH, b_rows, B_HEADS, B_HD)),
        'cache_b_v': nrm(5, (DEC_BATCH, b_rows, B_HEADS, B_HD)),
        'cache_c_latent': nrm(6, (DEC_BATCH, PAST_LEN, C_KV_LORA)),
        'cache_c_krope': nrm(7, (DEC_BATCH, PAST_LEN, C_ROPE)),
        'cache_d_k': nrm(8, (DEC_BATCH, PAST_LEN, D_HEADS, D_HD)),
        'cache_d_v': nrm(9, (DEC_BATCH, PAST_LEN, D_HEADS, D_HD)),
        'g_pre0': gain(10, D_MODEL),
        'w_in0': nrm(11, (D_MODEL, EVEN_IN), D_MODEL ** -0.5),
        'lam_q1': nrm(12, (A_QK_DIM,), 0.1),
        'lam_k1': nrm(13, (A_QK_DIM,), 0.1),
        'lam_q2': nrm(14, (A_QK_DIM,), 0.1),
        'lam_k2': nrm(15, (A_QK_DIM,), 0.1),
        'g_sub_a': gain(16, A_V_DIM),
        'rel_bias_b': nrm(17, (B_HEADS, 2 * B_REL_CLIP + 1), 0.1),
        'w_out0': nrm(18, (MIX_EVEN, D_MODEL), MIX_EVEN ** -0.5),
        'g_post0': gain(19, D_MODEL),
        'g_pre1': gain(20, D_MODEL),
        'w_in1': nrm(21, (D_MODEL, ODD_IN), D_MODEL ** -0.5),
        'g_cq': gain(22, C_Q_LORA),
        'w_uq': nrm(23, (C_Q_LORA, C_HEADS * (C_NOPE + C_ROPE)), C_Q_LORA ** -0.5),
        'g_ckv': gain(24, C_KV_LORA),
        'w_uk': nrm(25, (C_KV_LORA, C_HEADS, C_NOPE), C_KV_LORA ** -0.5),
        'w_uv': nrm(26, (C_KV_LORA, C_HEADS, C_V), C_KV_LORA ** -0.5),
        'w_out1': nrm(27, (MIX_ODD, D_MODEL), MIX_ODD ** -0.5),
        'g_post1': gain(28, D_MODEL),
    }


def reference(x_prompt, x_sample, cache_a_k, cache_a_v, cache_b_k, cache_b_v, cache_c_latent, cache_c_krope,
              cache_d_k, cache_d_v, g_pre0, w_in0, lam_q1, lam_k1, lam_q2, lam_k2, g_sub_a, rel_bias_b, w_out0,
              g_post0, g_pre1, w_in1, g_cq, w_uq, g_ckv, w_uk, w_uv, w_out1, g_post1):
    past = cache_a_k.shape[1]
    pos_p = jnp.arange(x_prompt.shape[1])
    pos_s = past + jnp.arange(x_sample.shape[1])
    hp, hs = x_prompt, x_sample
    for layer in range(DEPTH):
        if layer % 2 == 0:
            hp, st_even_p = layer_even(hp, pos_p, None, layer, g_pre0, w_in0, lam_q1, lam_k1, lam_q2, lam_k2,
                                       g_sub_a, rel_bias_b, w_out0, g_post0)
            hs, st_even_s = layer_even(hs, pos_s, (cache_a_k, cache_a_v, cache_b_k, cache_b_v), layer,
                                       g_pre0, w_in0, lam_q1, lam_k1, lam_q2, lam_k2, g_sub_a, rel_bias_b,
                                       w_out0, g_post0)
        else:
            hp, st_odd_p = layer_odd(hp, pos_p, None, g_pre1, w_in1, g_cq, w_uq, g_ckv, w_uk, w_uv, w_out1, g_post1)
            hs, st_odd_s = layer_odd(hs, pos_s, (cache_c_latent, cache_c_krope, cache_d_k, cache_d_v),
                                     g_pre1, w_in1, g_cq, w_uq, g_ckv, w_uk, w_uv, w_out1, g_post1)
    a_k_p, a_v_p, b_k_p, b_v_p = st_even_p
    a_k_s, a_v_s, b_k_s, b_v_s = st_even_s
    c_lat_p, c_krope_p, d_k_p, d_v_p = st_odd_p
    c_lat_s, c_krope_s, d_k_s, d_v_s = st_odd_s
    return (hp, hs, a_k_p, a_v_p, b_k_p, b_v_p, c_lat_p, c_krope_p, d_k_p, d_v_p,
            a_k_s, a_v_s, b_k_s, b_v_s, c_lat_s, c_krope_s, d_k_s, d_v_s)
```

```python
import functools
import math

import jax
import jax.numpy as jnp
from jax import lax
from jax.experimental import pallas as pl
from jax.experimental.pallas import tpu as pltpu

F32 = jnp.float32
BF16 = jnp.bfloat16

D_MODEL = 1024
CHUNK = 64
ROPE_THETA = 500000.0
NORM_EPS = 1e-6
A_HEADS, A_QK, A_V, A_ROT = 4, 64, 128, 16
B_HEADS, B_HD, B_LEFT, B_CLIP = 8, 64, 8, 128
C_HEADS, C_NOPE, C_ROPE, C_V, C_QL, C_KVL = 8, 64, 32, 64, 768, 256
D_HEADS, D_HD = 8, 64
B_BAND = B_LEFT * CHUNK
LANES = 128
NEG = -1e30
VMEM_LIMIT = 56 * 1024 * 1024

TQ = 256
TQ_B = 128
WIN_B = B_BAND + TQ_B


def _dot(a, b):
    return jnp.dot(a, b, preferred_element_type=F32)


def _dot_nt(a, b):
    return lax.dot_general(a, b, (((1,), (1,)), ((), ())), preferred_element_type=F32)


def _rms(x, g):
    ms = jnp.mean(x * x, axis=-1, keepdims=True)
    return x * lax.rsqrt(ms + NORM_EPS) * g


def _silu(x):
    return x * (1.0 / (1.0 + jnp.exp(-x)))


def _softplus(z):
    return jnp.maximum(z, 0.0) + jnp.log(1.0 + jnp.exp(-jnp.abs(z)))


def _rope(z, c, s1, s2, shift):
    return z * c + pltpu.roll(z, LANES - shift, 1) * s1 + pltpu.roll(z, shift, 1) * s2


def _lane_lo():
    return lax.broadcasted_iota(jnp.int32, (1, LANES), 1) < 64


def _params(sem):
    return pltpu.CompilerParams(dimension_semantics=sem, vmem_limit_bytes=VMEM_LIMIT)


def _proj0_kernel(x_ref, g_ref, w_ref, rc_ref, rs1_ref, rs2_ref,
                  aq_ref, ak_ref, av_ref, bq_ref, bk_ref, bv_ref, sg_ref):
    u = _rms(x_ref[0], g_ref[...]).astype(BF16)
    c, s1, s2 = rc_ref[...], rs1_ref[...], rs2_ref[...]
    zq = _dot(u, w_ref[:, 0:512])
    zk = _dot(u, w_ref[:, 512:1024])
    for v in range(4):
        sl = slice(v * LANES, (v + 1) * LANES)
        aq_ref[0, :, sl] = (_rope(zq[:, sl], c, s1, s2, A_ROT // 2) * (A_QK ** -0.5)).astype(BF16)
        ak_ref[0, :, sl] = _rope(zk[:, sl], c, s1, s2, A_ROT // 2)
    av_ref[0] = _dot(u, w_ref[:, 1024:1536])
    bq_ref[0] = (_dot(u, w_ref[:, 1536:2048]) * (B_HD ** -0.5)).astype(BF16)
    bk_ref[0] = _dot(u, w_ref[:, 2048:2560])
    bv_ref[0] = _dot(u, w_ref[:, 2560:3072])
    sg_ref[0] = _silu(_dot(u, w_ref[:, 3072:4096])).astype(BF16)


def _proj0(x, g, w, tabs, tm):
    b, t, _ = x.shape
    tok = lambda n: pl.BlockSpec((1, tm, n), lambda ti, bi: (bi, ti, 0))
    const = lambda shape: pl.BlockSpec(shape, lambda ti, bi: (0,) * len(shape))
    tab = pl.BlockSpec((tm, LANES), lambda ti, bi: (ti, 0))
    widths = (512, 512, 512, 512, 512, 512, 1024)
    dts = (BF16, F32, F32, BF16, F32, F32, BF16)
    return pl.pallas_call(
        _proj0_kernel,
        grid=(t // tm, b),
        in_specs=[tok(D_MODEL), const((1, D_MODEL)), const(w.shape), tab, tab, tab],
        out_specs=[tok(n) for n in widths],
        out_shape=[jax.ShapeDtypeStruct((b, t, n), d) for n, d in zip(widths, dts)],
        compiler_params=_params(("arbitrary", "arbitrary")),
        name="proj0",
    )(x, g, w, *tabs)


def _proj1_kernel(x_ref, g_ref, w_ref, gcq_ref, wuq_ref, gckv_ref, rc_ref, rs1_ref, rs2_ref,
                  qc_ref, clat_ref, kr_ref, dq_ref, dk_ref, dv_ref, sg_ref):
    u = _rms(x_ref[0], g_ref[...]).astype(BF16)
    c, s1, s2 = rc_ref[...], rs1_ref[...], rs2_ref[...]
    cq = _rms(_dot(u, w_ref[:, 0:768]), gcq_ref[...]).astype(BF16)
    qc = _dot(cq, wuq_ref[...])
    for h in range(C_HEADS):
        sl = slice(h * LANES, (h + 1) * LANES)
        qc_ref[0, :, sl] = (_rope(qc[:, sl], c, s1, s2, C_ROPE // 2)
                            * ((C_NOPE + C_ROPE) ** -0.5)).astype(BF16)
    clat_ref[0] = _rms(_dot(u, w_ref[:, 768:1024]), gckv_ref[...])
    dq_ref[0] = (_dot(u, w_ref[:, 1024:1536]) * (D_HD ** -0.5)).astype(BF16)
    dk_ref[0] = _dot(u, w_ref[:, 1536:2048])
    dv_ref[0] = _dot(u, w_ref[:, 2048:2560])
    sg_ref[0] = _silu(_dot(u, w_ref[:, 2560:3584])).astype(BF16)
    kr_ref[0] = _rope(_dot(u, w_ref[:, 3584:3712]), c, s1, s2, C_ROPE // 2)


def _proj1(x, g, w, gcq, wuq, gckv, tabs, tm):
    b, t, _ = x.shape
    tok = lambda n: pl.BlockSpec((1, tm, n), lambda ti, bi: (bi, ti, 0))
    const = lambda shape: pl.BlockSpec(shape, lambda ti, bi: (0,) * len(shape))
    tab = pl.BlockSpec((tm, LANES), lambda ti, bi: (ti, 0))
    widths = (1024, 256, 128, 512, 512, 512, 1024)
    dts = (BF16, F32, F32, BF16, F32, F32, BF16)
    return pl.pallas_call(
        _proj1_kernel,
        grid=(t // tm, b),
        in_specs=[tok(D_MODEL), const((1, D_MODEL)), const(w.shape), const((1, C_QL)),
                  const(wuq.shape), const((1, C_KVL)), tab, tab, tab],
        out_specs=[tok(n) for n in widths],
        out_shape=[jax.ShapeDtypeStruct((b, t, n), d) for n, d in zip(widths, dts)],
        compiler_params=_params(("arbitrary", "arbitrary")),
        name="proj1",
    )(x, g, w, gcq, wuq, gckv, *tabs)


def _outproj_kernel(o1_ref, o2_ref, w_ref, g_ref, h_ref, out_ref):
    y = _dot(o1_ref[...], w_ref[0:512, :]) + _dot(o2_ref[...], w_ref[512:1024, :])
    out_ref[...] = h_ref[...] + _rms(y, g_ref[...])


def _outproj(o1, o2, w, g, h, tm):
    n = h.shape[0]
    row = lambda c: pl.BlockSpec((tm, c), lambda i: (i, 0))
    const = lambda shape: pl.BlockSpec(shape, lambda i: (0,) * len(shape))
    return pl.pallas_call(
        _outproj_kernel,
        grid=(n // tm,),
        in_specs=[row(512), row(512), const(w.shape), const((1, D_MODEL)), row(D_MODEL)],
        out_specs=row(D_MODEL),
        out_shape=jax.ShapeDtypeStruct((n, D_MODEL), F32),
        compiler_params=_params(("arbitrary",)),
        name="outproj",
    )(o1, o2, w, g, h)


def _bias_kernel(rb_ref, out_ref):
    h = pl.program_id(0)
    i = lax.broadcasted_iota(jnp.int32, (TQ_B, WIN_B), 0)
    j = lax.broadcasted_iota(jnp.int32, (TQ_B, WIN_B), 1)
    idx = jnp.clip(B_BAND + i - j, -B_CLIP, B_CLIP) + B_CLIP
    ci = lax.shift_right_logical(i, 6)
    cj = lax.shift_right_logical(j, 6)
    band = jnp.logical_and(cj >= ci, cj <= ci + B_LEFT)

    def body(r, acc):
        return jnp.where(idx == r, rb_ref[h, r], acc)

    acc = lax.fori_loop(0, 2 * B_CLIP + 1, body, jnp.zeros((TQ_B, WIN_B), F32))
    out_ref[0] = jnp.where(band, acc, NEG)


def _bias_tile(rel_bias):
    return pl.pallas_call(
        _bias_kernel,
        grid=(B_HEADS,),
        in_specs=[pl.BlockSpec(memory_space=pltpu.SMEM)],
        out_specs=pl.BlockSpec((1, TQ_B, WIN_B), lambda h: (h, 0, 0)),
        out_shape=jax.ShapeDtypeStruct((B_HEADS, TQ_B, WIN_B), F32),
        compiler_params=_params(("arbitrary",)),
        name="bias_tile",
    )(rel_bias)


def _chunk_mask(tq):
    r = lax.broadcasted_iota(jnp.int32, (tq, tq), 0)
    c = lax.broadcasted_iota(jnp.int32, (tq, tq), 1)
    return lax.shift_right_logical(c, 6) <= lax.shift_right_logical(r, 6)


def _softmax_update(carry, s, v):
    m, l, acc = carry
    m_new = jnp.maximum(m, jnp.max(s, axis=1, keepdims=True))
    alpha = jnp.exp(m - m_new)
    p = jnp.exp(s - m_new)
    l = alpha * l + jnp.sum(p, axis=1, keepdims=True)
    acc = alpha * acc + _dot(p.astype(BF16), v)
    return m_new, l, acc


def _causal_softmax_attend(q, k_ref, v_ref, qt, tq, mask):
    def body(j, carry):
        rows = pl.ds(pl.multiple_of(j * tq, tq), tq)
        return _softmax_update(carry, _dot_nt(q, k_ref[rows, :]), v_ref[rows, :])

    init = (jnp.full((tq, 1), NEG, F32), jnp.zeros((tq, 1), F32), jnp.zeros((tq, LANES), F32))
    carry = lax.fori_loop(0, qt, body, init)
    rows = pl.ds(pl.multiple_of(qt * tq, tq), tq)
    s = jnp.where(mask, _dot_nt(q, k_ref[rows, :]), NEG)
    _, l, acc = _softmax_update(carry, s, v_ref[rows, :])
    return acc * (1.0 / l)


def _lam(lq1, lk1, lq2, lk2, lam_init):
    return (jnp.exp(jnp.sum(lq1[...] * lk1[...], axis=1, keepdims=True))
            - jnp.exp(jnp.sum(lq2[...] * lk2[...], axis=1, keepdims=True)) + lam_init)


def _attn_a_kernel(q_ref, k_ref, v_ref, lq1, lk1, lq2, lk2, gsub_ref, sg_ref, o_ref,
                   kb_ref, vb_ref, *, tq, lam_init):
    qt = pl.program_id(2)

    @pl.when(qt == 0)
    def _():
        kb_ref[...] = k_ref[0].astype(BF16)
        vb_ref[...] = v_ref[0].astype(BF16)

    q = q_ref[0]
    lo = _lane_lo()
    mask = _chunk_mask(tq)
    zero = jnp.zeros_like(q)
    o0 = _causal_softmax_attend(jnp.where(lo, q, zero), kb_ref, vb_ref, qt, tq, mask)
    o1 = _causal_softmax_attend(jnp.where(lo, zero, q), kb_ref, vb_ref, qt, tq, mask)
    o = o0 - _lam(lq1, lk1, lq2, lk2, lam_init) * o1
    o = _rms(o, gsub_ref[...]) * (1.0 - lam_init)
    o_ref[0] = (o * sg_ref[0].astype(F32)).astype(BF16)


def _attn_a(q, k, v, lams, gsub, sg, lam_init):
    b, t, _ = q.shape
    tq = min(TQ, t)
    qblk = pl.BlockSpec((1, tq, LANES), lambda bi, h, qi: (bi, qi, h))
    kvblk = pl.BlockSpec((1, t, LANES), lambda bi, h, qi: (bi, 0, h))
    vec = lambda n: pl.BlockSpec((1, n), lambda bi, h, qi: (0, 0))
    return pl.pallas_call(
        functools.partial(_attn_a_kernel, tq=tq, lam_init=lam_init),
        grid=(b, A_HEADS, t // tq),
        in_specs=[qblk, kvblk, kvblk, vec(A_QK), vec(A_QK), vec(A_QK), vec(A_QK), vec(A_V), qblk],
        out_specs=qblk,
        out_shape=jax.ShapeDtypeStruct((b, t, A_HEADS * A_V), BF16),
        scratch_shapes=[pltpu.VMEM((t, LANES), BF16), pltpu.VMEM((t, LANES), BF16)],
        compiler_params=_params(("arbitrary", "arbitrary", "arbitrary")),
        name="attn_a",
    )(q, k, v, *lams, gsub, sg)


def _attn_b_kernel(q_ref, k_ref, v_ref, bias_ref, sg_ref, o_ref, kb_ref, vb_ref, *, t):
    qt = pl.program_id(2)

    @pl.when(qt == 0)
    def _():
        kb_ref[0:B_BAND, :] = jnp.zeros((B_BAND, LANES), BF16)
        vb_ref[0:B_BAND, :] = jnp.zeros((B_BAND, LANES), BF16)
        kb_ref[B_BAND:B_BAND + t, :] = k_ref[0].astype(BF16)
        vb_ref[B_BAND:B_BAND + t, :] = v_ref[0].astype(BF16)

    q0 = pl.multiple_of(qt * TQ_B, TQ_B)
    kw = kb_ref[pl.ds(q0, WIN_B), :]
    vw = vb_ref[pl.ds(q0, WIN_B), :]
    col = lax.broadcasted_iota(jnp.int32, (1, WIN_B), 1)
    pen = jnp.where(col < B_BAND - q0, NEG, 0.0)
    q = q_ref[0]
    lo = _lane_lo()
    zero = jnp.zeros_like(q)
    outs = []
    for hh in range(2):
        qm = jnp.where(lo, q, zero) if hh == 0 else jnp.where(lo, zero, q)
        s = _dot_nt(qm, kw) + bias_ref[hh] + pen
        p = jnp.exp(s - jnp.max(s, axis=1, keepdims=True))
        l = jnp.sum(p, axis=1, keepdims=True)
        outs.append(_dot(p.astype(BF16), vw) * (1.0 / l))
    o = jnp.where(lo, outs[0], outs[1])
    o_ref[0] = (o * sg_ref[0].astype(F32)).astype(BF16)


def _attn_b(q, k, v, bias, sg):
    b, t, _ = q.shape
    assert t % TQ_B == 0
    qblk = pl.BlockSpec((1, TQ_B, LANES), lambda bi, h, qi: (bi, qi, h))
    sgblk = pl.BlockSpec((1, TQ_B, LANES), lambda bi, h, qi: (bi, qi, h + A_HEADS))
    kvblk = pl.BlockSpec((1, t, LANES), lambda bi, h, qi: (bi, 0, h))
    bblk = pl.BlockSpec((2, TQ_B, WIN_B), lambda bi, h, qi: (h, 0, 0))
    return pl.pallas_call(
        functools.partial(_attn_b_kernel, t=t),
        grid=(b, B_HEADS // 2, t // TQ_B),
        in_specs=[qblk, kvblk, kvblk, bblk, sgblk],
        out_specs=qblk,
        out_shape=jax.ShapeDtypeStruct((b, t, B_HEADS * B_HD), BF16),
        scratch_shapes=[pltpu.VMEM((B_BAND + t, LANES), BF16), pltpu.VMEM((B_BAND + t, LANES), BF16)],
        compiler_params=_params(("arbitrary", "arbitrary", "arbitrary")),
        name="attn_b",
    )(q, k, v, bias, sg)


def _attn_c_kernel(q_ref, clat_ref, kr_ref, wuk_ref, wuv_ref, sg_ref, o_ref, kc_ref, vb_ref, *, tq):
    qt = pl.program_id(2)

    @pl.when(qt == 0)
    def _():
        cl = clat_ref[0].astype(BF16)
        kr = kr_ref[0]
        for hh in range(2):
            kc_ref[hh] = (_dot(cl, wuk_ref[hh]) + kr).astype(BF16)
        vb_ref[...] = _dot(cl, wuv_ref[0]).astype(BF16)

    mask = _chunk_mask(tq)
    o0 = _causal_softmax_attend(q_ref[0, :, 0:LANES], kc_ref.at[0], vb_ref, qt, tq, mask)
    o1 = _causal_softmax_attend(q_ref[0, :, LANES:2 * LANES], kc_ref.at[1], vb_ref, qt, tq, mask)
    o = jnp.where(_lane_lo(), o0, o1)
    o_ref[0] = (o * sg_ref[0].astype(F32)).astype(BF16)


def _attn_c(qc, clat, kr, wuk, wuv, sg):
    b, t, _ = qc.shape
    tq = min(TQ, t)
    qblk = pl.BlockSpec((1, tq, 2 * LANES), lambda bi, h, qi: (bi, qi, h))
    oblk = pl.BlockSpec((1, tq, LANES), lambda bi, h, qi: (bi, qi, h))
    return pl.pallas_call(
        functools.partial(_attn_c_kernel, tq=tq),
        grid=(b, C_HEADS // 2, t // tq),
        in_specs=[qblk,
                  pl.BlockSpec((1, t, C_KVL), lambda bi, h, qi: (bi, 0, 0)),
                  pl.BlockSpec((1, t, LANES), lambda bi, h, qi: (bi, 0, 0)),
                  pl.BlockSpec((2, C_KVL, LANES), lambda bi, h, qi: (h, 0, 0)),
                  pl.BlockSpec((1, C_KVL, LANES), lambda bi, h, qi: (h, 0, 0)),
                  oblk],
        out_specs=oblk,
        out_shape=jax.ShapeDtypeStruct((b, t, C_HEADS * C_V), BF16),
        scratch_shapes=[pltpu.VMEM((2, t, LANES), BF16), pltpu.VMEM((t, LANES), BF16)],
        compiler_params=_params(("arbitrary", "arbitrary", "arbitrary")),
        name="attn_c",
    )(qc, clat, kr, wuk, wuv, sg)


def _suffix_ones(n):
    r = lax.broadcasted_iota(jnp.int32, (n, n), 0)
    c = lax.broadcasted_iota(jnp.int32, (n, n), 1)
    return jnp.where(r > c, 1.0, 0.0).astype(BF16)


def _stick_tile(q, k, v, u, run, acc, tri):
    z = _dot_nt(q, k)
    sp = _softplus(z)
    lm = -sp if tri is None else jnp.where(tri, -sp, 0.0)
    later = _dot(lm.astype(BF16), u)
    a = jnp.exp(z - sp + later + run)
    if tri is not None:
        a = jnp.where(tri, a, 0.0)
    acc = acc + _dot(a.astype(BF16), v)
    run = run + later[:, 0:1] + lm[:, 0:1]
    return run, acc


def _strict_lower(n):
    r = lax.broadcasted_iota(jnp.int32, (n, n), 0)
    c = lax.broadcasted_iota(jnp.int32, (n, n), 1)
    return c < r


def _attn_d_kernel(q_ref, k_ref, v_ref, sg_ref, o_ref, kb_ref, vb_ref, *, tq):
    qt = pl.program_id(2)

    @pl.when(qt == 0)
    def _():
        kb_ref[...] = k_ref[0].astype(BF16)
        vb_ref[...] = v_ref[0].astype(BF16)

    q = q_ref[0]
    lo = _lane_lo()
    zero = jnp.zeros_like(q)
    u = _suffix_ones(tq)
    tri = _strict_lower(tq)
    outs = []
    for hh in range(2):
        qm = jnp.where(lo, q, zero) if hh == 0 else jnp.where(lo, zero, q)
        rows = pl.ds(pl.multiple_of(qt * tq, tq), tq)
        run, acc = _stick_tile(qm, kb_ref[rows, :], vb_ref[rows, :], u,
                               jnp.zeros((tq, 1), F32), jnp.zeros((tq, LANES), F32), tri)

        def body(i, carry, qm=qm):
            rows = pl.ds(pl.multiple_of((qt - 1 - i) * tq, tq), tq)
            return _stick_tile(qm, kb_ref[rows, :], vb_ref[rows, :], u, carry[0], carry[1], None)

        _, acc = lax.fori_loop(0, qt, body, (run, acc))
        outs.append(acc)
    o = jnp.where(lo, outs[0], outs[1])
    o_ref[0] = (o * sg_ref[0].astype(F32)).astype(BF16)


def _attn_d(q, k, v, sg):
    b, t, _ = q.shape
    tq = min(TQ, t)
    qblk = pl.BlockSpec((1, tq, LANES), lambda bi, h, qi: (bi, qi, h))
    sgblk = pl.BlockSpec((1, tq, LANES), lambda bi, h, qi: (bi, qi, h + C_HEADS // 2))
    kvblk = pl.BlockSpec((1, t, LANES), lambda bi, h, qi: (bi, 0, h))
    return pl.pallas_call(
        functools.partial(_attn_d_kernel, tq=tq),
        grid=(b, D_HEADS // 2, t // tq),
        in_specs=[qblk, kvblk, kvblk, sgblk],
        out_specs=qblk,
        out_shape=jax.ShapeDtypeStruct((b, t, D_HEADS * D_HD), BF16),
        scratch_shapes=[pltpu.VMEM((t, LANES), BF16), pltpu.VMEM((t, LANES), BF16)],
        compiler_params=_params(("arbitrary", "arbitrary", "arbitrary")),
        name="attn_d",
    )(q, k, v, sg)


def _two_part_softmax(q, kc, kn, vc, vn, bc=None, bn=None):
    sc = _dot_nt(q, kc)
    sn = _dot_nt(q, kn)
    if bc is not None:
        sc = sc + bc
        sn = sn + bn
    m = jnp.maximum(jnp.max(sc, axis=1, keepdims=True), jnp.max(sn, axis=1, keepdims=True))
    pc = jnp.exp(sc - m)
    pn = jnp.exp(sn - m)
    l = jnp.sum(pc, axis=1, keepdims=True) + jnp.sum(pn, axis=1, keepdims=True)
    return (_dot(pc.astype(BF16), vc) + _dot(pn.astype(BF16), vn)) * (1.0 / l)


def _samp_a_kernel(q_ref, ck_ref, cv_ref, nk_ref, nv_ref, lq1, lk1, lq2, lk2, gsub_ref, sg_ref, o_ref,
                   *, lam_init):
    lo = _lane_lo()
    lam = _lam(lq1, lk1, lq2, lk2, lam_init)
    for h in range(A_HEADS):
        sl = slice(h * LANES, (h + 1) * LANES)
        kc, vc = ck_ref[0, :, sl].astype(BF16), cv_ref[0, :, sl].astype(BF16)
        kn, vn = nk_ref[0, :, sl].astype(BF16), nv_ref[0, :, sl].astype(BF16)
        q = q_ref[0, :, sl]
        zero = jnp.zeros_like(q)
        o0 = _two_part_softmax(jnp.where(lo, q, zero), kc, kn, vc, vn)
        o1 = _two_part_softmax(jnp.where(lo, zero, q), kc, kn, vc, vn)
        o = _rms(o0 - lam * o1, gsub_ref[...]) * (1.0 - lam_init)
        o_ref[0, :, sl] = (o * sg_ref[0, :, sl].astype(F32)).astype(BF16)


def _samp_b_kernel(q_ref, ck_ref, cv_ref, nk_ref, nv_ref, bias_ref, sg_ref, o_ref, *, band, st):
    lo = _lane_lo()
    for hp in range(B_HEADS // 2):
        sl = slice(hp * LANES, (hp + 1) * LANES)
        kc, vc = ck_ref[0, :, sl].astype(BF16), cv_ref[0, :, sl].astype(BF16)
        kn, vn = nk_ref[0, :, sl].astype(BF16), nv_ref[0, :, sl].astype(BF16)
        q = q_ref[0, :, sl]
        zero = jnp.zeros_like(q)
        outs = []
        for hh in range(2):
            qm = jnp.where(lo, q, zero) if hh == 0 else jnp.where(lo, zero, q)
            bias = bias_ref[2 * hp + hh]
            outs.append(_two_part_softmax(qm, kc, kn, vc, vn, bias[:, 0:band], bias[:, band:band + st]))
        o = jnp.where(lo, outs[0], outs[1])
        o_ref[0, :, sl] = (o * sg_ref[0, :, sl].astype(F32)).astype(BF16)


def _samp_c_kernel(q_ref, clat_ref, krp_ref, nclat_ref, nkr_ref, wuk_ref, wuv_ref, sg_ref, o_ref):
    lo = _lane_lo()
    cl = clat_ref[0].astype(BF16)
    ncl = nclat_ref[0].astype(BF16)
    krc = krp_ref[0].astype(F32)
    krn = nkr_ref[0]
    for hp in range(C_HEADS // 2):
        sl = slice(hp * LANES, (hp + 1) * LANES)
        vc = _dot(cl, wuv_ref[hp]).astype(BF16)
        vn = _dot(ncl, wuv_ref[hp]).astype(BF16)
        outs = []
        for hh in range(2):
            h = 2 * hp + hh
            kc = (_dot(cl, wuk_ref[h]) + krc).astype(BF16)
            kn = (_dot(ncl, wuk_ref[h]) + krn).astype(BF16)
            outs.append(_two_part_softmax(q_ref[0, :, h * LANES:(h + 1) * LANES], kc, kn, vc, vn))
        o = jnp.where(lo, outs[0], outs[1])
        o_ref[0, :, sl] = (o * sg_ref[0, :, sl].astype(F32)).astype(BF16)


def _samp_d_kernel(q_ref, ck_ref, cv_ref, nk_ref, nv_ref, sg_ref, o_ref, *, past, st, tk):
    lo = _lane_lo()
    u_new = _suffix_ones(st)
    tri = _strict_lower(st)
    u = _suffix_ones(tk)
    for hp in range(D_HEADS // 2):
        sl = slice(hp * LANES, (hp + 1) * LANES)
        q = q_ref[0, :, sl]
        zero = jnp.zeros_like(q)
        kn, vn = nk_ref[0, :, sl].astype(BF16), nv_ref[0, :, sl].astype(BF16)
        outs = []
        for hh in range(2):
            qm = jnp.where(lo, q, zero) if hh == 0 else jnp.where(lo, zero, q)
            run, acc = _stick_tile(qm, kn, vn, u_new, jnp.zeros((st, 1), F32),
                                   jnp.zeros((st, LANES), F32), tri)
            for j in reversed(range(past // tk)):
                kc = ck_ref[0, j * tk:(j + 1) * tk, sl].astype(BF16)
                vc = cv_ref[0, j * tk:(j + 1) * tk, sl].astype(BF16)
                run, acc = _stick_tile(qm, kc, vc, u, run, acc, None)
            outs.append(acc)
        o = jnp.where(lo, outs[0], outs[1])
        o_ref[0, :, sl] = (o * sg_ref[0, :, sl].astype(F32)).astype(BF16)


def _samp_call(body, name, b, st, ins, specs):
    return pl.pallas_call(
        body,
        grid=(b,),
        in_specs=specs,
        out_specs=pl.BlockSpec((1, st, 512), lambda bi: (bi, 0, 0)),
        out_shape=jax.ShapeDtypeStruct((b, st, 512), BF16),
        compiler_params=_params(("arbitrary",)),
        name=name,
    )(*ins)


def _per_batch(shape, col=0):
    return pl.BlockSpec((1,) + tuple(shape[1:]), lambda bi: (bi,) + (0,) * (len(shape) - 2) + (col,))


def _whole(shape):
    return pl.BlockSpec(tuple(shape), lambda bi: (0,) * len(shape))


def _rope_tables(pos, rot, period, offset):
    half = rot // 2
    inv_freq = ROPE_THETA ** (-jnp.arange(half, dtype=F32) / half)
    ang = pos.astype(F32)[:, None] * inv_freq[None, :]
    cos, sin = jnp.cos(ang), jnp.sin(ang)
    n = pos.shape[0]
    c = jnp.ones((n, period), F32).at[:, offset:offset + half].set(cos).at[:, offset + half:offset + rot].set(cos)
    s1 = jnp.zeros((n, period), F32).at[:, offset:offset + half].set(-sin)
    s2 = jnp.zeros((n, period), F32).at[:, offset + half:offset + rot].set(sin)
    reps = LANES // period
    return tuple(jnp.tile(x, (1, reps)) for x in (c, s1, s2))


def kernel(x_prompt, x_sample, cache_a_k, cache_a_v, cache_b_k, cache_b_v, cache_c_latent, cache_c_krope,
           cache_d_k, cache_d_v, g_pre0, w_in0, lam_q1, lam_k1, lam_q2, lam_k2, g_sub_a, rel_bias_b, w_out0,
           g_post0, g_pre1, w_in1, g_cq, w_uq, g_ckv, w_uk, w_uv, w_out1, g_post1):
    b, t, _ = x_prompt.shape
    sb, st, _ = x_sample.shape
    past = cache_a_k.shape[1]
    band = cache_b_k.shape[1]
    assert band == B_BAND and t % TQ == 0 and past % TQ == 0
    ns = sb * st
    row = lambda x: x.reshape(1, -1)

    w0 = w_in0.astype(BF16)
    wo0 = w_out0.astype(BF16)
    wo1 = w_out1.astype(BF16)
    w1 = jnp.concatenate([w_in1[:, 0:1024], w_in1[:, 1056:3616], jnp.zeros((D_MODEL, 64), F32),
                          w_in1[:, 1024:1056], jnp.zeros((D_MODEL, 32), F32)], axis=1).astype(BF16)
    wuq = jnp.pad(w_uq.reshape(C_QL, C_HEADS, C_NOPE + C_ROPE), ((0, 0), (0, 0), (0, 32)))
    wuq = wuq.reshape(C_QL, C_HEADS * LANES).astype(BF16)
    wuk = jnp.pad(w_uk.transpose(1, 0, 2), ((0, 0), (0, 0), (0, LANES - C_NOPE))).astype(BF16)
    wuv = w_uv.reshape(C_KVL, C_HEADS // 2, 2 * C_V).transpose(1, 0, 2).astype(BF16)
    lams = tuple(row(x) for x in (lam_q1, lam_k1, lam_q2, lam_k2))
    lam_init = 0.8 - 0.6 * math.exp(-0.3 * 0)

    pos_p = jnp.arange(t)
    pos_s = past + jnp.arange(st)
    tabs0_p = _rope_tables(pos_p, A_ROT, A_QK, 0)
    tabs0_s = tuple(jnp.tile(x, (sb, 1)) for x in _rope_tables(pos_s, A_ROT, A_QK, 0))
    tabs1_p = _rope_tables(pos_p, C_ROPE, LANES, C_NOPE)
    tabs1_s = tuple(jnp.tile(x, (sb, 1)) for x in _rope_tables(pos_s, C_ROPE, LANES, C_NOPE))

    bias = _bias_tile(rel_bias_b)

    aq, ak, av, bq, bk, bv, sg0 = _proj0(x_prompt, row(g_pre0), w0, tabs0_p, 256)
    oa = _attn_a(aq, ak, av, lams, row(g_sub_a), sg0, lam_init)
    ob = _attn_b(bq, bk, bv, bias, sg0)
    h1 = _outproj(oa.reshape(b * t, 512), ob.reshape(b * t, 512), wo0, row(g_post0),
                  x_prompt.reshape(b * t, D_MODEL), 512)

    xs = x_sample.reshape(1, ns, D_MODEL)
    s_out = _proj0(xs, row(g_pre0), w0, tabs0_s, ns)
    aq_s, ak_s, av_s, bq_s, bk_s, bv_s, sg0_s = (x.reshape(sb, st, -1) for x in s_out)
    vec = lambda n: _whole((1, n))
    oa_s = _samp_call(
        functools.partial(_samp_a_kernel, lam_init=lam_init), "samp_a", sb, st,
        (aq_s, cache_a_k.reshape(sb, past, 512), cache_a_v.reshape(sb, past, 512), ak_s, av_s,
         *lams, row(g_sub_a), sg0_s),
        [_per_batch((sb, st, 512)), _per_batch((sb, past, 512)), _per_batch((sb, past, 512)),
         _per_batch((sb, st, 512)), _per_batch((sb, st, 512)), vec(A_QK), vec(A_QK), vec(A_QK), vec(A_QK),
         vec(A_V), _per_batch((sb, st, 512))])
    ob_s = _samp_call(
        functools.partial(_samp_b_kernel, band=band, st=st), "samp_b", sb, st,
        (bq_s, cache_b_k.reshape(sb, band, 512), cache_b_v.reshape(sb, band, 512), bk_s, bv_s, bias, sg0_s),
        [_per_batch((sb, st, 512)), _per_batch((sb, band, 512)), _per_batch((sb, band, 512)),
         _per_batch((sb, st, 512)), _per_batch((sb, st, 512)),
         pl.BlockSpec((B_HEADS, st, WIN_B), lambda bi: (0, 0, 0)), _per_batch((sb, st, 512), col=1)])
    hs1 = _outproj(oa_s.reshape(ns, 512), ob_s.reshape(ns, 512), wo0, row(g_post0),
                   x_sample.reshape(ns, D_MODEL), ns)

    qc, clat, kr, dq, dk, dv, sg1 = _proj1(h1.reshape(b, t, D_MODEL), row(g_pre1), w1, row(g_cq), wuq,
                                           row(g_ckv), tabs1_p, 256)
    oc = _attn_c(qc, clat, kr, wuk, wuv, sg1)
    od = _attn_d(dq, dk, dv, sg1)
    h2 = _outproj(oc.reshape(b * t, 512), od.reshape(b * t, 512), wo1, row(g_post1), h1, 512)

    s_out = _proj1(hs1.reshape(1, ns, D_MODEL), row(g_pre1), w1, row(g_cq), wuq, row(g_ckv), tabs1_s, ns)
    qc_s, clat_s, kr_s, dq_s, dk_s, dv_s, sg1_s = (x.reshape(sb, st, -1) for x in s_out)
    krp = jnp.pad(cache_c_krope, ((0, 0), (0, 0), (C_NOPE, LANES - C_NOPE - C_ROPE))).astype(BF16)
    oc_s = _samp_call(
        _samp_c_kernel, "samp_c", sb, st,
        (qc_s, cache_c_latent, krp, clat_s, kr_s, wuk, wuv, sg1_s),
        [_per_batch((sb, st, 1024)), _per_batch((sb, past, C_KVL)), _per_batch((sb, past, LANES)),
         _per_batch((sb, st, C_KVL)), _per_batch((sb, st, LANES)), _whole(wuk.shape), _whole(wuv.shape),
         _per_batch((sb, st, 512))])
    od_s = _samp_call(
        functools.partial(_samp_d_kernel, past=past, st=st, tk=TQ), "samp_d", sb, st,
        (dq_s, cache_d_k.reshape(sb, past, 512), cache_d_v.reshape(sb, past, 512), dk_s, dv_s, sg1_s),
        [_per_batch((sb, st, 512)), _per_batch((sb, past, 512)), _per_batch((sb, past, 512)),
         _per_batch((sb, st, 512)), _per_batch((sb, st, 512)), _per_batch((sb, st, 512), col=1)])
    hs2 = _outproj(oc_s.reshape(ns, 512), od_s.reshape(ns, 512), wo1, row(g_post1), hs1, ns)

    b_rows = min(B_BAND, t)
    return (h2.reshape(b, t, D_MODEL), hs2.reshape(sb, st, D_MODEL),
            ak.reshape(b, t, A_HEADS, 2, A_QK), av.reshape(b, t, A_HEADS, A_V),
            bk[:, t - b_rows:].reshape(b, b_rows, B_HEADS, B_HD), bv[:, t - b_rows:].reshape(b, b_rows, B_HEADS, B_HD),
            clat, kr[:, :, C_NOPE:C_NOPE + C_ROPE],
            dk.reshape(b, t, D_HEADS, D_HD), dv.reshape(b, t, D_HEADS, D_HD),
            ak_s.reshape(sb, st, A_HEADS, 2, A_QK), av_s.reshape(sb, st, A_HEADS, A_V),
            bk_s.reshape(sb, st, B_HEADS, B_HD), bv_s.reshape(sb, st, B_HEADS, B_HD),
            clat_s, kr_s[:, :, C_NOPE:C_NOPE + C_ROPE],
            dk_s.reshape(sb, st, D_HEADS, D_HD), dv_s.reshape(sb, st, D_HEADS, D_HD))
```

```python
import functools
import math

import jax
import jax.numpy as jnp
from jax import lax
from jax.experimental import pallas as pl
from jax.experimental.pallas import tpu as pltpu

F32 = jnp.float32
BF16 = jnp.bfloat16

D_MODEL = 1024
CHUNK = 64
ROPE_THETA = 500000.0
NORM_EPS = 1e-6
A_HEADS, A_QK, A_V, A_ROT = 4, 64, 128, 16
B_HEADS, B_HD, B_LEFT, B_CLIP = 8, 64, 8, 128
C_HEADS, C_NOPE, C_ROPE, C_V, C_QL, C_KVL = 8, 64, 32, 64, 768, 256
D_HEADS, D_HD = 8, 64
B_BAND = B_LEFT * CHUNK
LANES = 128
NEG = -1e30
VMEM_LIMIT = 56 * 1024 * 1024

TQ = 512
TQ_D = 256
TQ_B = 128
WIN_B = B_BAND + TQ_B
LOG2E = 1.4426950408889634


def _dot(a, b):
    return jnp.dot(a, b, preferred_element_type=F32)


def _dot_nt(a, b):
    return lax.dot_general(a, b, (((1,), (1,)), ((), ())), preferred_element_type=F32)


def _rms(x, g):
    ms = jnp.mean(x * x, axis=-1, keepdims=True)
    return x * lax.rsqrt(ms + NORM_EPS) * g


def _silu(x):
    return x * (1.0 / (1.0 + jnp.exp(-x)))


def _softplus(z):
    return jnp.maximum(z, 0.0) + jnp.log(1.0 + jnp.exp(-jnp.abs(z)))


def _rope(z, c, s1, s2, shift):
    return z * c + pltpu.roll(z, LANES - shift, 1) * s1 + pltpu.roll(z, shift, 1) * s2


def _lane_lo():
    return lax.broadcasted_iota(jnp.int32, (1, LANES), 1) < 64


def _params(sem):
    return pltpu.CompilerParams(dimension_semantics=sem, vmem_limit_bytes=VMEM_LIMIT)


def _proj0_kernel(x_ref, g_ref, w_ref, rc_ref, rs1_ref, rs2_ref,
                  aq_ref, ak_ref, av_ref, bq_ref, bk_ref, bv_ref, sg_ref):
    u = _rms(x_ref[0], g_ref[...]).astype(BF16)
    c, s1, s2 = rc_ref[...], rs1_ref[...], rs2_ref[...]
    zq = _dot(u, w_ref[:, 0:512])
    zk = _dot(u, w_ref[:, 512:1024])
    for v in range(4):
        sl = slice(v * LANES, (v + 1) * LANES)
        aq_ref[0, :, sl] = (_rope(zq[:, sl], c, s1, s2, A_ROT // 2) * (A_QK ** -0.5 * LOG2E)).astype(BF16)
        ak_ref[0, :, sl] = _rope(zk[:, sl], c, s1, s2, A_ROT // 2)
    av_ref[0] = _dot(u, w_ref[:, 1024:1536])
    bq_ref[0] = (_dot(u, w_ref[:, 1536:2048]) * (B_HD ** -0.5 * LOG2E)).astype(BF16)
    bk_ref[0] = _dot(u, w_ref[:, 2048:2560])
    bv_ref[0] = _dot(u, w_ref[:, 2560:3072])
    sg_ref[0] = _silu(_dot(u, w_ref[:, 3072:4096])).astype(BF16)


def _proj0(x, g, w, tabs, tm):
    b, t, _ = x.shape
    tok = lambda n: pl.BlockSpec((1, tm, n), lambda ti, bi: (bi, ti, 0))
    const = lambda shape: pl.BlockSpec(shape, lambda ti, bi: (0,) * len(shape))
    tab = pl.BlockSpec((tm, LANES), lambda ti, bi: (ti, 0))
    widths = (512, 512, 512, 512, 512, 512, 1024)
    dts = (BF16, F32, F32, BF16, F32, F32, BF16)
    return pl.pallas_call(
        _proj0_kernel,
        grid=(t // tm, b),
        in_specs=[tok(D_MODEL), const((1, D_MODEL)), const(w.shape), tab, tab, tab],
        out_specs=[tok(n) for n in widths],
        out_shape=[jax.ShapeDtypeStruct((b, t, n), d) for n, d in zip(widths, dts)],
        compiler_params=_params(("arbitrary", "arbitrary")),
        name="proj0",
    )(x, g, w, *tabs)


def _proj1_kernel(x_ref, g_ref, w_ref, gcq_ref, wuq_ref, gckv_ref, rc_ref, rs1_ref, rs2_ref,
                  qc_ref, clat_ref, kr_ref, dq_ref, dk_ref, dv_ref, sg_ref):
    u = _rms(x_ref[0], g_ref[...]).astype(BF16)
    c, s1, s2 = rc_ref[...], rs1_ref[...], rs2_ref[...]
    cq = _rms(_dot(u, w_ref[:, 0:768]), gcq_ref[...]).astype(BF16)
    qc = _dot(cq, wuq_ref[...])
    for h in range(C_HEADS):
        sl = slice(h * LANES, (h + 1) * LANES)
        qc_ref[0, :, sl] = (_rope(qc[:, sl], c, s1, s2, C_ROPE // 2)
                            * ((C_NOPE + C_ROPE) ** -0.5 * LOG2E)).astype(BF16)
    clat_ref[0] = _rms(_dot(u, w_ref[:, 768:1024]), gckv_ref[...])
    dq_ref[0] = (_dot(u, w_ref[:, 1024:1536]) * (D_HD ** -0.5)).astype(BF16)
    dk_ref[0] = _dot(u, w_ref[:, 1536:2048])
    dv_ref[0] = _dot(u, w_ref[:, 2048:2560])
    sg_ref[0] = _silu(_dot(u, w_ref[:, 2560:3584])).astype(BF16)
    kr_ref[0] = _rope(_dot(u, w_ref[:, 3584:3712]), c, s1, s2, C_ROPE // 2)


def _proj1(x, g, w, gcq, wuq, gckv, tabs, tm):
    b, t, _ = x.shape
    tok = lambda n: pl.BlockSpec((1, tm, n), lambda ti, bi: (bi, ti, 0))
    const = lambda shape: pl.BlockSpec(shape, lambda ti, bi: (0,) * len(shape))
    tab = pl.BlockSpec((tm, LANES), lambda ti, bi: (ti, 0))
    widths = (1024, 256, 128, 512, 512, 512, 1024)
    dts = (BF16, F32, F32, BF16, F32, F32, BF16)
    return pl.pallas_call(
        _proj1_kernel,
        grid=(t // tm, b),
        in_specs=[tok(D_MODEL), const((1, D_MODEL)), const(w.shape), const((1, C_QL)),
                  const(wuq.shape), const((1, C_KVL)), tab, tab, tab],
        out_specs=[tok(n) for n in widths],
        out_shape=[jax.ShapeDtypeStruct((b, t, n), d) for n, d in zip(widths, dts)],
        compiler_params=_params(("arbitrary", "arbitrary")),
        name="proj1",
    )(x, g, w, gcq, wuq, gckv, *tabs)


def _outproj_kernel(o1_ref, o2_ref, w_ref, g_ref, h_ref, out_ref):
    y = _dot(o1_ref[...], w_ref[0:512, :]) + _dot(o2_ref[...], w_ref[512:1024, :])
    out_ref[...] = h_ref[...] + _rms(y, g_ref[...])


def _outproj(o1, o2, w, g, h, tm):
    n = h.shape[0]
    row = lambda c: pl.BlockSpec((tm, c), lambda i: (i, 0))
    const = lambda shape: pl.BlockSpec(shape, lambda i: (0,) * len(shape))
    return pl.pallas_call(
        _outproj_kernel,
        grid=(n // tm,),
        in_specs=[row(512), row(512), const(w.shape), const((1, D_MODEL)), row(D_MODEL)],
        out_specs=row(D_MODEL),
        out_shape=jax.ShapeDtypeStruct((n, D_MODEL), F32),
        compiler_params=_params(("arbitrary",)),
        name="outproj",
    )(o1, o2, w, g, h)


def _bias_kernel(rb_ref, out_ref):
    h = pl.program_id(0)
    i = lax.broadcasted_iota(jnp.int32, (TQ_B, WIN_B), 0)
    j = lax.broadcasted_iota(jnp.int32, (TQ_B, WIN_B), 1)
    idx = jnp.clip(B_BAND + i - j, -B_CLIP, B_CLIP) + B_CLIP
    ci = lax.shift_right_logical(i, 6)
    cj = lax.shift_right_logical(j, 6)
    band = jnp.logical_and(cj >= ci, cj <= ci + B_LEFT)

    def body(r, acc):
        return jnp.where(idx == r, rb_ref[h, r], acc)

    acc = lax.fori_loop(0, 2 * B_CLIP + 1, body, jnp.zeros((TQ_B, WIN_B), F32))
    out_ref[0] = jnp.where(band, acc * LOG2E, NEG)


def _bias_tile(rel_bias):
    return pl.pallas_call(
        _bias_kernel,
        grid=(B_HEADS,),
        in_specs=[pl.BlockSpec(memory_space=pltpu.SMEM)],
        out_specs=pl.BlockSpec((1, TQ_B, WIN_B), lambda h: (h, 0, 0)),
        out_shape=jax.ShapeDtypeStruct((B_HEADS, TQ_B, WIN_B), F32),
        compiler_params=_params(("arbitrary",)),
        name="bias_tile",
    )(rel_bias)


def _chunk_mask(tq):
    r = lax.broadcasted_iota(jnp.int32, (tq, tq), 0)
    c = lax.broadcasted_iota(jnp.int32, (tq, tq), 1)
    return lax.shift_right_logical(c, 6) <= lax.shift_right_logical(r, 6)


def _lane_col(x, lane):
    idx = lax.broadcasted_iota(jnp.int32, (1, LANES), 1)
    return jnp.sum(jnp.where(idx == lane, x, 0.0), axis=1, keepdims=True)


def _softmax2_update(carry, s, v, sum_in_v):
    m, l, acc = carry
    m_new = jnp.maximum(m, jnp.max(s, axis=1, keepdims=True))
    alpha = jnp.exp2(m - m_new)
    p = jnp.exp2(s - m_new)
    if not sum_in_v:
        l = alpha * l + jnp.sum(p, axis=1, keepdims=True)
    acc = alpha * acc + _dot(p.astype(BF16), v)
    return m_new, l, acc


def _causal_pair_attend(qs, k_refs, v_refs, qt, tq, mask, sum_in_v):
    n = len(qs)

    def step(carry, rows, msk):
        out = []
        for c in range(n):
            s = _dot_nt(qs[c], k_refs[c][rows, :])
            if msk is not None:
                s = jnp.where(msk, s, NEG)
            out.append(_softmax2_update(carry[c], s, v_refs[c][rows, :], sum_in_v))
        return tuple(out)

    init = tuple((jnp.full((tq, 1), NEG, F32), jnp.zeros((tq, 1), F32), jnp.zeros((tq, LANES), F32))
                 for _ in range(n))
    carry = lax.fori_loop(
        0, qt, lambda j, cr: step(cr, pl.ds(pl.multiple_of(j * tq, tq), tq), None), init)
    carry = step(carry, pl.ds(pl.multiple_of(qt * tq, tq), tq), mask)
    outs = []
    for _, l, acc in carry:
        if sum_in_v:
            l = _lane_col(acc, 64)
        outs.append(acc * (1.0 / l))
    return outs


def _lam(lq1, lk1, lq2, lk2, lam_init):
    return (jnp.exp(jnp.sum(lq1[...] * lk1[...], axis=1, keepdims=True))
            - jnp.exp(jnp.sum(lq2[...] * lk2[...], axis=1, keepdims=True)) + lam_init)


def _head_split(q):
    lo = _lane_lo()
    zero = jnp.zeros_like(q)
    return [jnp.where(lo, q, zero), jnp.where(lo, zero, q)]


def _attn_a_kernel(q_ref, k_ref, v_ref, lq1, lk1, lq2, lk2, gsub_ref, sg_ref, o_ref,
                   kb_ref, vb_ref, *, tq, lam_init):
    qt = pl.program_id(2)

    @pl.when(qt == 0)
    def _():
        kb_ref[...] = k_ref[0].astype(BF16)
        vb_ref[...] = v_ref[0].astype(BF16)

    o0, o1 = _causal_pair_attend(_head_split(q_ref[0]), [kb_ref, kb_ref], [vb_ref, vb_ref], qt, tq,
                                 _chunk_mask(tq), False)
    o = o0 - _lam(lq1, lk1, lq2, lk2, lam_init) * o1
    o = _rms(o, gsub_ref[...]) * (1.0 - lam_init)
    o_ref[0] = (o * sg_ref[0].astype(F32)).astype(BF16)


def _attn_a(q, k, v, lams, gsub, sg, lam_init):
    b, t, _ = q.shape
    tq = min(TQ, t)
    qblk = pl.BlockSpec((1, tq, LANES), lambda bi, h, qi: (bi, qi, h))
    kvblk = pl.BlockSpec((1, t, LANES), lambda bi, h, qi: (bi, 0, h))
    vec = lambda n: pl.BlockSpec((1, n), lambda bi, h, qi: (0, 0))
    return pl.pallas_call(
        functools.partial(_attn_a_kernel, tq=tq, lam_init=lam_init),
        grid=(b, A_HEADS, t // tq),
        in_specs=[qblk, kvblk, kvblk, vec(A_QK), vec(A_QK), vec(A_QK), vec(A_QK), vec(A_V), qblk],
        out_specs=qblk,
        out_shape=jax.ShapeDtypeStruct((b, t, A_HEADS * A_V), BF16),
        scratch_shapes=[pltpu.VMEM((t, LANES), BF16), pltpu.VMEM((t, LANES), BF16)],
        compiler_params=_params(("arbitrary", "arbitrary", "arbitrary")),
        name="attn_a",
    )(q, k, v, *lams, gsub, sg)


def _ones_hi(v, lo):
    return jnp.where(lo, v, 1.0).astype(BF16)


def _attn_b_kernel(q_ref, k_ref, v_ref, bias_ref, sg_ref, o_ref, kb_ref, vx_ref, *, t):
    lo = _lane_lo()
    kb_ref[0:B_BAND, :] = jnp.zeros((B_BAND, LANES), BF16)
    kb_ref[B_BAND:B_BAND + t, :] = k_ref[0].astype(BF16)
    v = v_ref[0]
    for hh in range(2):
        vx_ref[hh, 0:B_BAND, :] = jnp.zeros((B_BAND, LANES), BF16)
    vx_ref[0, B_BAND:B_BAND + t, :] = _ones_hi(v, lo)
    vx_ref[1, B_BAND:B_BAND + t, :] = _ones_hi(pltpu.roll(v, 64, 1), lo)
    col = lax.broadcasted_iota(jnp.int32, (1, WIN_B), 1)

    def tile(q0, first_valid):
        win = pl.ds(q0, WIN_B)
        rows = pl.ds(q0, TQ_B)
        kw = kb_ref[win, :]
        outs = []
        for hh, qm in enumerate(_head_split(q_ref[0, rows, :])):
            s = _dot_nt(qm, kw) + bias_ref[hh]
            if first_valid > 0:
                s = s + jnp.where(col < first_valid, NEG, 0.0)
            p = jnp.exp2(s - jnp.max(s, axis=1, keepdims=True))
            acc = _dot(p.astype(BF16), vx_ref[hh, win, :])
            outs.append(acc * (1.0 / _lane_col(acc, 64)))
        o = jnp.where(lo, outs[0], pltpu.roll(outs[1], 64, 1))
        o_ref[0, rows, :] = (o * sg_ref[0, rows, :].astype(F32)).astype(BF16)

    n_edge = min(B_BAND, t) // TQ_B
    for qi in range(n_edge):
        tile(qi * TQ_B, B_BAND - qi * TQ_B)

    def body(i, carry):
        q0 = pl.multiple_of(B_BAND + i * 2 * TQ_B, 2 * TQ_B)
        tile(q0, 0)
        tile(q0 + TQ_B, 0)
        return carry

    lax.fori_loop(0, (t - n_edge * TQ_B) // (2 * TQ_B), body, 0)


def _attn_b(q, k, v, bias, sg):
    b, t, _ = q.shape
    assert t % (2 * TQ_B) == 0
    blk = pl.BlockSpec((1, t, LANES), lambda bi, h: (bi, 0, h))
    sgblk = pl.BlockSpec((1, t, LANES), lambda bi, h: (bi, 0, h + A_HEADS))
    bblk = pl.BlockSpec((2, TQ_B, WIN_B), lambda bi, h: (h, 0, 0))
    return pl.pallas_call(
        functools.partial(_attn_b_kernel, t=t),
        grid=(b, B_HEADS // 2),
        in_specs=[blk, blk, blk, bblk, sgblk],
        out_specs=blk,
        out_shape=jax.ShapeDtypeStruct((b, t, B_HEADS * B_HD), BF16),
        scratch_shapes=[pltpu.VMEM((B_BAND + t, LANES), BF16), pltpu.VMEM((2, B_BAND + t, LANES), BF16)],
        compiler_params=_params(("arbitrary", "arbitrary")),
        name="attn_b",
    )(q, k, v, bias, sg)


def _attn_c_kernel(q_ref, clat_ref, kr_ref, wuk_ref, wuv_ref, sg_ref, o_ref, kc_ref, vx_ref, *, tq):
    qt = pl.program_id(2)
    lo = _lane_lo()

    @pl.when(qt == 0)
    def _():
        cl = clat_ref[0].astype(BF16)
        kr = kr_ref[0]
        for hh in range(2):
            kc_ref[hh] = (_dot(cl, wuk_ref[hh]) + kr).astype(BF16)
            vx_ref[hh] = _ones_hi(_dot(cl, wuv_ref[hh]), lo)

    o0, o1 = _causal_pair_attend([q_ref[0, :, 0:LANES], q_ref[0, :, LANES:2 * LANES]],
                                 [kc_ref.at[0], kc_ref.at[1]], [vx_ref.at[0], vx_ref.at[1]], qt, tq,
                                 _chunk_mask(tq), True)
    o = jnp.where(lo, o0, pltpu.roll(o1, 64, 1))
    o_ref[0] = (o * sg_ref[0].astype(F32)).astype(BF16)


def _attn_c(qc, clat, kr, wuk, wuv, sg):
    b, t, _ = qc.shape
    tq = min(TQ, t)
    qblk = pl.BlockSpec((1, tq, 2 * LANES), lambda bi, h, qi: (bi, qi, h))
    oblk = pl.BlockSpec((1, tq, LANES), lambda bi, h, qi: (bi, qi, h))
    wblk = pl.BlockSpec((2, C_KVL, LANES), lambda bi, h, qi: (h, 0, 0))
    return pl.pallas_call(
        functools.partial(_attn_c_kernel, tq=tq),
        grid=(b, C_HEADS // 2, t // tq),
        in_specs=[qblk,
                  pl.BlockSpec((1, t, C_KVL), lambda bi, h, qi: (bi, 0, 0)),
                  pl.BlockSpec((1, t, LANES), lambda bi, h, qi: (bi, 0, 0)),
                  wblk, wblk, oblk],
        out_specs=oblk,
        out_shape=jax.ShapeDtypeStruct((b, t, C_HEADS * C_V), BF16),
        scratch_shapes=[pltpu.VMEM((2, t, LANES), BF16), pltpu.VMEM((2, t, LANES), BF16)],
        compiler_params=_params(("arbitrary", "arbitrary", "arbitrary")),
        name="attn_c",
    )(qc, clat, kr, wuk, wuv, sg)


def _stick_unit(q, k, v, negu, run, tri):
    z = _dot_nt(q, k)
    sp = _softplus(z)
    msp = sp if tri is None else jnp.where(tri, sp, 0.0)
    later = _dot(msp.astype(BF16), negu)
    x = z - sp + later
    if run is not None:
        x = x + run
    a = jnp.exp(x)
    if tri is not None:
        a = jnp.where(tri, a, 0.0)
    total = later[:, 0:1] - msp[:, 0:1]
    return total, _dot(a.astype(BF16), v)


def _attn_d_kernel(q_ref, k_ref, v_ref, negu_ref, sg_ref, o_ref, kb_ref, vb_ref, *, tq, t):
    qt = pl.program_id(2)

    @pl.when(qt == 0)
    def _():
        kb_ref[0:tq, :] = jnp.zeros((tq, LANES), BF16)
        vb_ref[0:tq, :] = jnp.zeros((tq, LANES), BF16)
        kb_ref[tq:tq + t, :] = k_ref[0].astype(BF16)
        vb_ref[tq:tq + t, :] = v_ref[0].astype(BF16)

    qs = _head_split(q_ref[0])
    r = lax.broadcasted_iota(jnp.int32, (tq, 2 * tq), 0)
    c = lax.broadcasted_iota(jnp.int32, (tq, 2 * tq), 1)
    tri = c < r + tq
    win = pl.ds(pl.multiple_of(qt * tq, tq), 2 * tq)
    kw, vw = kb_ref[win, :], vb_ref[win, :]
    first = [_stick_unit(q, kw, vw, negu_ref[...], None, tri) for q in qs]
    run0, acc0 = first[0]
    run1, acc1 = first[1]

    def cond(carry):
        return jnp.logical_and(carry[0] < qt - 1, carry[1] > -104.0)

    def body(carry):
        i, _, run0, acc0, run1, acc1 = carry
        rows = pl.ds(pl.multiple_of((qt - 1 - i) * tq, tq), tq)
        k, v = kb_ref[rows, :], vb_ref[rows, :]
        negu = negu_ref[0:tq, 0:tq]
        d0, c0 = _stick_unit(qs[0], k, v, negu, run0, None)
        d1, c1 = _stick_unit(qs[1], k, v, negu, run1, None)
        run0, run1 = run0 + d0, run1 + d1
        return i + 1, jnp.maximum(jnp.max(run0), jnp.max(run1)), run0, acc0 + c0, run1, acc1 + c1

    init = (jnp.int32(0), jnp.maximum(jnp.max(run0), jnp.max(run1)), run0, acc0, run1, acc1)
    _, _, _, acc0, _, acc1 = lax.while_loop(cond, body, init)
    o = jnp.where(_lane_lo(), acc0, acc1)
    o_ref[0] = (o * sg_ref[0].astype(F32)).astype(BF16)


def _attn_d(q, k, v, negu, sg):
    b, t, _ = q.shape
    tq = min(TQ_D, t)
    qblk = pl.BlockSpec((1, tq, LANES), lambda bi, h, qi: (bi, qi, h))
    sgblk = pl.BlockSpec((1, tq, LANES), lambda bi, h, qi: (bi, qi, h + C_HEADS // 2))
    kvblk = pl.BlockSpec((1, t, LANES), lambda bi, h, qi: (bi, 0, h))
    ublk = pl.BlockSpec((2 * tq, 2 * tq), lambda bi, h, qi: (0, 0))
    return pl.pallas_call(
        functools.partial(_attn_d_kernel, tq=tq, t=t),
        grid=(b, D_HEADS // 2, t // tq),
        in_specs=[qblk, kvblk, kvblk, ublk, sgblk],
        out_specs=qblk,
        out_shape=jax.ShapeDtypeStruct((b, t, D_HEADS * D_HD), BF16),
        scratch_shapes=[pltpu.VMEM((tq + t, LANES), BF16), pltpu.VMEM((tq + t, LANES), BF16)],
        compiler_params=_params(("arbitrary", "arbitrary", "arbitrary")),
        name="attn_d",
    )(q, k, v, negu, sg)


def _two_part_softmax(q, kc, kn, vc, vn, bc=None, bn=None):
    sc = _dot_nt(q, kc)
    sn = _dot_nt(q, kn)
    if bc is not None:
        sc = sc + bc
        sn = sn + bn
    m = jnp.maximum(jnp.max(sc, axis=1, keepdims=True), jnp.max(sn, axis=1, keepdims=True))
    pc = jnp.exp2(sc - m)
    pn = jnp.exp2(sn - m)
    l = jnp.sum(pc, axis=1, keepdims=True) + jnp.sum(pn, axis=1, keepdims=True)
    return (_dot(pc.astype(BF16), vc) + _dot(pn.astype(BF16), vn)) * (1.0 / l)


def _samp_a_kernel(q_ref, ck_ref, cv_ref, nk_ref, nv_ref, lq1, lk1, lq2, lk2, gsub_ref, sg_ref, o_ref,
                   *, lam_init):
    lam = _lam(lq1, lk1, lq2, lk2, lam_init)
    for h in range(A_HEADS):
        sl = slice(h * LANES, (h + 1) * LANES)
        kc, vc = ck_ref[0, :, sl].astype(BF16), cv_ref[0, :, sl].astype(BF16)
        kn, vn = nk_ref[0, :, sl].astype(BF16), nv_ref[0, :, sl].astype(BF16)
        q0, q1 = _head_split(q_ref[0, :, sl])
        o0 = _two_part_softmax(q0, kc, kn, vc, vn)
        o1 = _two_part_softmax(q1, kc, kn, vc, vn)
        o = _rms(o0 - lam * o1, gsub_ref[...]) * (1.0 - lam_init)
        o_ref[0, :, sl] = (o * sg_ref[0, :, sl].astype(F32)).astype(BF16)


def _samp_b_kernel(q_ref, ck_ref, cv_ref, nk_ref, nv_ref, bias_ref, sg_ref, o_ref, *, band, st):
    lo = _lane_lo()
    for hp in range(B_HEADS // 2):
        sl = slice(hp * LANES, (hp + 1) * LANES)
        kc, vc = ck_ref[0, :, sl].astype(BF16), cv_ref[0, :, sl].astype(BF16)
        kn, vn = nk_ref[0, :, sl].astype(BF16), nv_ref[0, :, sl].astype(BF16)
        outs = []
        for hh, qm in enumerate(_head_split(q_ref[0, :, sl])):
            bias = bias_ref[2 * hp + hh]
            outs.append(_two_part_softmax(qm, kc, kn, vc, vn, bias[:, 0:band], bias[:, band:band + st]))
        o = jnp.where(lo, outs[0], outs[1])
        o_ref[0, :, sl] = (o * sg_ref[0, :, sl].astype(F32)).astype(BF16)


def _samp_c_kernel(q_ref, clat_ref, krp_ref, nclat_ref, nkr_ref, wuk_ref, wuv_ref, sg_ref, o_ref):
    lo = _lane_lo()
    cl = clat_ref[0].astype(BF16)
    ncl = nclat_ref[0].astype(BF16)
    krc = krp_ref[0].astype(F32)
    krn = nkr_ref[0]
    for hp in range(C_HEADS // 2):
        sl = slice(hp * LANES, (hp + 1) * LANES)
        vc = _dot(cl, wuv_ref[hp]).astype(BF16)
        vn = _dot(ncl, wuv_ref[hp]).astype(BF16)
        outs = []
        for hh in range(2):
            h = 2 * hp + hh
            kc = (_dot(cl, wuk_ref[h]) + krc).astype(BF16)
            kn = (_dot(ncl, wuk_ref[h]) + krn).astype(BF16)
            outs.append(_two_part_softmax(q_ref[0, :, h * LANES:(h + 1) * LANES], kc, kn, vc, vn))
        o = jnp.where(lo, outs[0], outs[1])
        o_ref[0, :, sl] = (o * sg_ref[0, :, sl].astype(F32)).astype(BF16)


def _neg_suffix(n):
    r = lax.broadcasted_iota(jnp.int32, (n, n), 0)
    c = lax.broadcasted_iota(jnp.int32, (n, n), 1)
    return jnp.where(r > c, -1.0, 0.0).astype(BF16)


def _samp_d_kernel(q_ref, ck_ref, cv_ref, nk_ref, nv_ref, sg_ref, o_ref, *, past, st, tk):
    lo = _lane_lo()
    negu_new = _neg_suffix(st)
    r = lax.broadcasted_iota(jnp.int32, (st, st), 0)
    c = lax.broadcasted_iota(jnp.int32, (st, st), 1)
    tri = c < r
    negu = _neg_suffix(tk)
    for hp in range(D_HEADS // 2):
        sl = slice(hp * LANES, (hp + 1) * LANES)
        kn, vn = nk_ref[0, :, sl].astype(BF16), nv_ref[0, :, sl].astype(BF16)
        outs = []
        for qm in _head_split(q_ref[0, :, sl]):
            run, acc = _stick_unit(qm, kn, vn, negu_new, None, tri)
            for j in reversed(range(past // tk)):
                kc = ck_ref[0, j * tk:(j + 1) * tk, sl].astype(BF16)
                vc = cv_ref[0, j * tk:(j + 1) * tk, sl].astype(BF16)
                d, contrib = _stick_unit(qm, kc, vc, negu, run, None)
                run, acc = run + d, acc + contrib
            outs.append(acc)
        o = jnp.where(lo, outs[0], outs[1])
        o_ref[0, :, sl] = (o * sg_ref[0, :, sl].astype(F32)).astype(BF16)


def _samp_call(body, name, b, st, ins, specs):
    return pl.pallas_call(
        body,
        grid=(b,),
        in_specs=specs,
        out_specs=pl.BlockSpec((1, st, 512), lambda bi: (bi, 0, 0)),
        out_shape=jax.ShapeDtypeStruct((b, st, 512), BF16),
        compiler_params=_params(("arbitrary",)),
        name=name,
    )(*ins)


def _per_batch(shape, col=0):
    return pl.BlockSpec((1,) + tuple(shape[1:]), lambda bi: (bi,) + (0,) * (len(shape) - 2) + (col,))


def _whole(shape):
    return pl.BlockSpec(tuple(shape), lambda bi: (0,) * len(shape))


def _rope_tables(pos, rot, period, offset):
    half = rot // 2
    inv_freq = ROPE_THETA ** (-jnp.arange(half, dtype=F32) / half)
    ang = pos.astype(F32)[:, None] * inv_freq[None, :]
    cos, sin = jnp.cos(ang), jnp.sin(ang)
    n = pos.shape[0]
    c = jnp.ones((n, period), F32).at[:, offset:offset + half].set(cos).at[:, offset + half:offset + rot].set(cos)
    s1 = jnp.zeros((n, period), F32).at[:, offset:offset + half].set(-sin)
    s2 = jnp.zeros((n, period), F32).at[:, offset + half:offset + rot].set(sin)
    reps = LANES // period
    return tuple(jnp.tile(x, (1, reps)) for x in (c, s1, s2))


def kernel(x_prompt, x_sample, cache_a_k, cache_a_v, cache_b_k, cache_b_v, cache_c_latent, cache_c_krope,
           cache_d_k, cache_d_v, g_pre0, w_in0, lam_q1, lam_k1, lam_q2, lam_k2, g_sub_a, rel_bias_b, w_out0,
           g_post0, g_pre1, w_in1, g_cq, w_uq, g_ckv, w_uk, w_uv, w_out1, g_post1):
    b, t, _ = x_prompt.shape
    sb, st, _ = x_sample.shape
    past = cache_a_k.shape[1]
    band = cache_b_k.shape[1]
    assert band == B_BAND and t % TQ == 0 and past % TQ_D == 0
    ns = sb * st
    row = lambda x: x.reshape(1, -1)

    w0 = w_in0.astype(BF16)
    wo0 = w_out0.astype(BF16)
    wo1 = w_out1.astype(BF16)
    w1 = jnp.concatenate([w_in1[:, 0:1024], w_in1[:, 1056:3616], jnp.zeros((D_MODEL, 64), F32),
                          w_in1[:, 1024:1056], jnp.zeros((D_MODEL, 32), F32)], axis=1).astype(BF16)
    wuq = jnp.pad(w_uq.reshape(C_QL, C_HEADS, C_NOPE + C_ROPE), ((0, 0), (0, 0), (0, 32)))
    wuq = wuq.reshape(C_QL, C_HEADS * LANES).astype(BF16)
    wuk = jnp.pad(w_uk.transpose(1, 0, 2), ((0, 0), (0, 0), (0, LANES - C_NOPE))).astype(BF16)
    wuv = w_uv.reshape(C_KVL, C_HEADS // 2, 2 * C_V).transpose(1, 0, 2).astype(BF16)
    wuv_pad = jnp.pad(w_uv.transpose(1, 0, 2), ((0, 0), (0, 0), (0, LANES - C_V))).astype(BF16)
    n_u = 2 * min(TQ_D, t)
    negu = -jnp.tril(jnp.ones((n_u, n_u), BF16), -1)
    lams = tuple(row(x) for x in (lam_q1, lam_k1, lam_q2, lam_k2))
    lam_init = 0.8 - 0.6 * math.exp(-0.3 * 0)

    pos_p = jnp.arange(t)
    pos_s = past + jnp.arange(st)
    tabs0_p = _rope_tables(pos_p, A_ROT, A_QK, 0)
    tabs0_s = tuple(jnp.tile(x, (sb, 1)) for x in _rope_tables(pos_s, A_ROT, A_QK, 0))
    tabs1_p = _rope_tables(pos_p, C_ROPE, LANES, C_NOPE)
    tabs1_s = tuple(jnp.tile(x, (sb, 1)) for x in _rope_tables(pos_s, C_ROPE, LANES, C_NOPE))

    bias = _bias_tile(rel_bias_b)

    aq, ak, av, bq, bk, bv, sg0 = _proj0(x_prompt, row(g_pre0), w0, tabs0_p, 256)
    oa = _attn_a(aq, ak, av, lams, row(g_sub_a), sg0, lam_init)
    ob = _attn_b(bq, bk, bv, bias, sg0)
    h1 = _outproj(oa.reshape(b * t, 512), ob.reshape(b * t, 512), wo0, row(g_post0),
                  x_prompt.reshape(b * t, D_MODEL), 512)

    xs = x_sample.reshape(1, ns, D_MODEL)
    s_out = _proj0(xs, row(g_pre0), w0, tabs0_s, ns)
    aq_s, ak_s, av_s, bq_s, bk_s, bv_s, sg0_s = (x.reshape(sb, st, -1) for x in s_out)
    vec = lambda n: _whole((1, n))
    oa_s = _samp_call(
        functools.partial(_samp_a_kernel, lam_init=lam_init), "samp_a", sb, st,
        (aq_s, cache_a_k.reshape(sb, past, 512), cache_a_v.reshape(sb, past, 512), ak_s, av_s,
         *lams, row(g_sub_a), sg0_s),
        [_per_batch((sb, st, 512)), _per_batch((sb, past, 512)), _per_batch((sb, past, 512)),
         _per_batch((sb, st, 512)), _per_batch((sb, st, 512)), vec(A_QK), vec(A_QK), vec(A_QK), vec(A_QK),
         vec(A_V), _per_batch((sb, st, 512))])
    ob_s = _samp_call(
        functools.partial(_samp_b_kernel, band=band, st=st), "samp_b", sb, st,
        (bq_s, cache_b_k.reshape(sb, band, 512), cache_b_v.reshape(sb, band, 512), bk_s, bv_s, bias, sg0_s),
        [_per_batch((sb, st, 512)), _per_batch((sb, band, 512)), _per_batch((sb, band, 512)),
         _per_batch((sb, st, 512)), _per_batch((sb, st, 512)),
         pl.BlockSpec((B_HEADS, st, WIN_B), lambda bi: (0, 0, 0)), _per_batch((sb, st, 512), col=1)])
    hs1 = _outproj(oa_s.reshape(ns, 512), ob_s.reshape(ns, 512), wo0, row(g_post0),
                   x_sample.reshape(ns, D_MODEL), ns)

    qc, clat, kr, dq, dk, dv, sg1 = _proj1(h1.reshape(b, t, D_MODEL), row(g_pre1), w1, row(g_cq), wuq,
                                           row(g_ckv), tabs1_p, 256)
    oc = _attn_c(qc, clat, kr, wuk, wuv_pad, sg1)
    od = _attn_d(dq, dk, dv, negu, sg1)
    h2 = _outproj(oc.reshape(b * t, 512), od.reshape(b * t, 512), wo1, row(g_post1), h1, 512)

    s_out = _proj1(hs1.reshape(1, ns, D_MODEL), row(g_pre1), w1, row(g_cq), wuq, row(g_ckv), tabs1_s, ns)
    qc_s, clat_s, kr_s, dq_s, dk_s, dv_s, sg1_s = (x.reshape(sb, st, -1) for x in s_out)
    krp = jnp.pad(cache_c_krope, ((0, 0), (0, 0), (C_NOPE, LANES - C_NOPE - C_ROPE))).astype(BF16)
    oc_s = _samp_call(
        _samp_c_kernel, "samp_c", sb, st,
        (qc_s, cache_c_latent, krp, clat_s, kr_s, wuk, wuv, sg1_s),
        [_per_batch((sb, st, 1024)), _per_batch((sb, past, C_KVL)), _per_batch((sb, past, LANES)),
         _per_batch((sb, st, C_KVL)), _per_batch((sb, st, LANES)), _whole(wuk.shape), _whole(wuv.shape),
         _per_batch((sb, st, 512))])
    od_s = _samp_call(
        functools.partial(_samp_d_kernel, past=past, st=st, tk=TQ_D), "samp_d", sb, st,
        (dq_s, cache_d_k.reshape(sb, past, 512), cache_d_v.reshape(sb, past, 512), dk_s, dv_s, sg1_s),
        [_per_batch((sb, st, 512)), _per_batch((sb, past, 512)), _per_batch((sb, past, 512)),
         _per_batch((sb, st, 512)), _per_batch((sb, st, 512)), _per_batch((sb, st, 512), col=1)])
    hs2 = _outproj(oc_s.reshape(ns, 512), od_s.reshape(ns, 512), wo1, row(g_post1), hs1, ns)

    b_rows = min(B_BAND, t)
    return (h2.reshape(b, t, D_MODEL), hs2.reshape(sb, st, D_MODEL),
            ak.reshape(b, t, A_HEADS, 2, A_QK), av.reshape(b, t, A_HEADS, A_V),
            bk[:, t - b_rows:].reshape(b, b_rows, B_HEADS, B_HD), bv[:, t - b_rows:].reshape(b, b_rows, B_HEADS, B_HD),
            clat, kr[:, :, C_NOPE:C_NOPE + C_ROPE],
            dk.reshape(b, t, D_HEADS, D_HD), dv.reshape(b, t, D_HEADS, D_HD),
            ak_s.reshape(sb, st, A_HEADS, 2, A_QK), av_s.reshape(sb, st, A_HEADS, A_V),
            bk_s.reshape(sb, st, B_HEADS, B_HD), bv_s.reshape(sb, st, B_HEADS, B_HD),
            clat_s, kr_s[:, :, C_NOPE:C_NOPE + C_ROPE],
            dk_s.reshape(sb, st, D_HEADS, D_HD), dv_s.reshape(sb, st, D_HEADS, D_HD))
```

```python
import functools
import math

import jax
import jax.numpy as jnp
from jax import lax
from jax.experimental import pallas as pl
from jax.experimental.pallas import tpu as pltpu

F32 = jnp.float32
BF16 = jnp.bfloat16

D_MODEL = 1024
CHUNK = 64
ROPE_THETA = 500000.0
NORM_EPS = 1e-6
A_HEADS, A_QK, A_V, A_ROT = 4, 64, 128, 16
B_HEADS, B_HD, B_LEFT, B_CLIP = 8, 64, 8, 128
C_HEADS, C_NOPE, C_ROPE, C_V, C_QL, C_KVL = 8, 64, 32, 64, 768, 256
D_HEADS, D_HD = 8, 64
B_BAND = B_LEFT * CHUNK
LANES = 128
NEG = -1e30
VMEM_LIMIT = 56 * 1024 * 1024

TQ = 512
TQ_D = 256
TQ_B = 128
WIN_B = B_BAND + TQ_B
LOG2E = 1.4426950408889634
V_ROWS = 144


def _dot(a, b):
    return jnp.dot(a, b, preferred_element_type=F32)


def _dot_nt(a, b):
    return lax.dot_general(a, b, (((1,), (1,)), ((), ())), preferred_element_type=F32)


def _rms(x, g):
    ms = jnp.mean(x * x, axis=-1, keepdims=True)
    return x * lax.rsqrt(ms + NORM_EPS) * g


def _silu(x):
    return x * (1.0 / (1.0 + jnp.exp(-x)))


def _softplus(z):
    return jnp.maximum(z, 0.0) + jnp.log(1.0 + jnp.exp(-jnp.abs(z)))


def _rope(z, c, s1, s2, shift):
    return z * c + pltpu.roll(z, LANES - shift, 1) * s1 + pltpu.roll(z, shift, 1) * s2


def _lane_lo():
    return lax.broadcasted_iota(jnp.int32, (1, LANES), 1) < 64


def _row_lo():
    return lax.broadcasted_iota(jnp.int32, (LANES, 1), 0) < 64


def _params(sem):
    return pltpu.CompilerParams(dimension_semantics=sem, vmem_limit_bytes=VMEM_LIMIT)


def _proj0_kernel(x_ref, g_ref, w_ref, rc_ref, rs1_ref, rs2_ref,
                  aq_ref, ak_ref, av_ref, bq_ref, bk_ref, bv_ref, sg_ref):
    u = _rms(x_ref[0], g_ref[...]).astype(BF16)
    c, s1, s2 = rc_ref[...], rs1_ref[...], rs2_ref[...]
    zq = _dot(u, w_ref[:, 0:512])
    zk = _dot(u, w_ref[:, 512:1024])
    for v in range(4):
        sl = slice(v * LANES, (v + 1) * LANES)
        aq_ref[0, :, sl] = (_rope(zq[:, sl], c, s1, s2, A_ROT // 2) * (A_QK ** -0.5 * LOG2E)).astype(BF16)
        ak_ref[0, :, sl] = _rope(zk[:, sl], c, s1, s2, A_ROT // 2)
    av_ref[0] = _dot(u, w_ref[:, 1024:1536])
    bq_ref[0] = (_dot(u, w_ref[:, 1536:2048]) * (B_HD ** -0.5 * LOG2E)).astype(BF16)
    bk_ref[0] = _dot(u, w_ref[:, 2048:2560])
    bv_ref[0] = _dot(u, w_ref[:, 2560:3072])
    sg_ref[0] = _silu(_dot(u, w_ref[:, 3072:4096])).astype(BF16)


def _proj0(x, g, w, tabs, tm):
    b, t, _ = x.shape
    tok = lambda n: pl.BlockSpec((1, tm, n), lambda ti, bi: (bi, ti, 0))
    const = lambda shape: pl.BlockSpec(shape, lambda ti, bi: (0,) * len(shape))
    tab = pl.BlockSpec((tm, LANES), lambda ti, bi: (ti, 0))
    widths = (512, 512, 512, 512, 512, 512, 1024)
    dts = (BF16, F32, F32, BF16, F32, F32, BF16)
    return pl.pallas_call(
        _proj0_kernel,
        grid=(t // tm, b),
        in_specs=[tok(D_MODEL), const((1, D_MODEL)), const(w.shape), tab, tab, tab],
        out_specs=[tok(n) for n in widths],
        out_shape=[jax.ShapeDtypeStruct((b, t, n), d) for n, d in zip(widths, dts)],
        compiler_params=_params(("arbitrary", "arbitrary")),
        name="proj0",
    )(x, g, w, *tabs)


def _proj1_kernel(x_ref, g_ref, w_ref, gcq_ref, wuq_ref, gckv_ref, rc_ref, rs1_ref, rs2_ref,
                  qc_ref, clat_ref, kr_ref, dq_ref, dk_ref, dv_ref, sg_ref):
    u = _rms(x_ref[0], g_ref[...]).astype(BF16)
    c, s1, s2 = rc_ref[...], rs1_ref[...], rs2_ref[...]
    cq = _rms(_dot(u, w_ref[:, 0:768]), gcq_ref[...]).astype(BF16)
    qc = _dot(cq, wuq_ref[...])
    for h in range(C_HEADS):
        sl = slice(h * LANES, (h + 1) * LANES)
        qc_ref[0, :, sl] = (_rope(qc[:, sl], c, s1, s2, C_ROPE // 2)
                            * ((C_NOPE + C_ROPE) ** -0.5 * LOG2E)).astype(BF16)
    clat_ref[0] = _rms(_dot(u, w_ref[:, 768:1024]), gckv_ref[...])
    dq_ref[0] = (_dot(u, w_ref[:, 1024:1536]) * (D_HD ** -0.5)).astype(BF16)
    dk_ref[0] = _dot(u, w_ref[:, 1536:2048])
    dv_ref[0] = _dot(u, w_ref[:, 2048:2560])
    sg_ref[0] = _silu(_dot(u, w_ref[:, 2560:3584])).astype(BF16)
    kr_ref[0] = _rope(_dot(u, w_ref[:, 3584:3712]), c, s1, s2, C_ROPE // 2)


def _proj1(x, g, w, gcq, wuq, gckv, tabs, tm):
    b, t, _ = x.shape
    tok = lambda n: pl.BlockSpec((1, tm, n), lambda ti, bi: (bi, ti, 0))
    const = lambda shape: pl.BlockSpec(shape, lambda ti, bi: (0,) * len(shape))
    tab = pl.BlockSpec((tm, LANES), lambda ti, bi: (ti, 0))
    widths = (1024, 256, 128, 512, 512, 512, 1024)
    dts = (BF16, F32, F32, BF16, F32, F32, BF16)
    return pl.pallas_call(
        _proj1_kernel,
        grid=(t // tm, b),
        in_specs=[tok(D_MODEL), const((1, D_MODEL)), const(w.shape), const((1, C_QL)),
                  const(wuq.shape), const((1, C_KVL)), tab, tab, tab],
        out_specs=[tok(n) for n in widths],
        out_shape=[jax.ShapeDtypeStruct((b, t, n), d) for n, d in zip(widths, dts)],
        compiler_params=_params(("arbitrary", "arbitrary")),
        name="proj1",
    )(x, g, w, gcq, wuq, gckv, *tabs)


def _outproj_kernel(o1_ref, o2_ref, w_ref, g_ref, h_ref, out_ref):
    y = _dot(o1_ref[...], w_ref[0:512, :]) + _dot(o2_ref[...], w_ref[512:1024, :])
    out_ref[...] = h_ref[...] + _rms(y, g_ref[...])


def _outproj(o1, o2, w, g, h, tm):
    n = h.shape[0]
    row = lambda c: pl.BlockSpec((tm, c), lambda i: (i, 0))
    const = lambda shape: pl.BlockSpec(shape, lambda i: (0,) * len(shape))
    return pl.pallas_call(
        _outproj_kernel,
        grid=(n // tm,),
        in_specs=[row(512), row(512), const(w.shape), const((1, D_MODEL)), row(D_MODEL)],
        out_specs=row(D_MODEL),
        out_shape=jax.ShapeDtypeStruct((n, D_MODEL), F32),
        compiler_params=_params(("arbitrary",)),
        name="outproj",
    )(o1, o2, w, g, h)


def _bias_lookup(rb_ref, h, idx):
    def body(r, acc):
        return jnp.where(idx == r, rb_ref[h, r], acc)

    return lax.fori_loop(0, 2 * B_CLIP + 1, body, jnp.zeros(idx.shape, F32)) * LOG2E


def _bias_kernel(rb_ref, out_t_ref, out_s_ref, *, st):
    h = pl.program_id(0)
    j = lax.broadcasted_iota(jnp.int32, (WIN_B, TQ_B), 0)
    i = lax.broadcasted_iota(jnp.int32, (WIN_B, TQ_B), 1)
    cj = lax.shift_right_logical(j, 6)
    ci = lax.shift_right_logical(i, 6)
    band = jnp.logical_and(cj >= ci, cj <= ci + B_LEFT)
    idx = jnp.clip(B_BAND + i - j, -B_CLIP, B_CLIP) + B_CLIP
    out_t_ref[0] = jnp.where(band, _bias_lookup(rb_ref, h, idx), NEG)
    i = lax.broadcasted_iota(jnp.int32, (st, WIN_B), 0)
    j = lax.broadcasted_iota(jnp.int32, (st, WIN_B), 1)
    out_s_ref[0] = _bias_lookup(rb_ref, h, jnp.clip(B_BAND + i - j, -B_CLIP, B_CLIP) + B_CLIP)


def _bias_tiles(rel_bias, st):
    return pl.pallas_call(
        functools.partial(_bias_kernel, st=st),
        grid=(B_HEADS,),
        in_specs=[pl.BlockSpec(memory_space=pltpu.SMEM)],
        out_specs=[pl.BlockSpec((1, WIN_B, TQ_B), lambda h: (h // 2, 0, h % 2)),
                   pl.BlockSpec((1, st, WIN_B), lambda h: (h, 0, 0))],
        out_shape=[jax.ShapeDtypeStruct((B_HEADS // 2, WIN_B, 2 * TQ_B), F32),
                   jax.ShapeDtypeStruct((B_HEADS, st, WIN_B), F32)],
        compiler_params=_params(("arbitrary",)),
        name="bias_tile",
    )(rel_bias)


def _chunk_mask_t(tq):
    j = lax.broadcasted_iota(jnp.int32, (tq, tq), 0)
    i = lax.broadcasted_iota(jnp.int32, (tq, tq), 1)
    return lax.shift_right_logical(j, 6) <= lax.shift_right_logical(i, 6)


def _ones_rows(n):
    r = lax.broadcasted_iota(jnp.int32, (V_ROWS - LANES, n), 0)
    return jnp.where(r == 0, 1.0, 0.0).astype(BF16)


def _causal_attend_all(n_tiles, n_chains, q_tile, k_tile, vt_tile, mask_t, finish):
    steps = [(qi, j) for qi in range(n_tiles) for j in range(qi + 1)]
    qcache = {}

    def scores(step):
        qi, j = step
        for c in range(n_chains):
            if (c, qi) not in qcache:
                qcache[(c, qi)] = q_tile(c, qi)
        return [_dot_nt(k_tile(c, j), qcache[(c, qi)]) for c in range(n_chains)]

    nxt = scores(steps[0])
    m, acc = [None] * n_chains, [None] * n_chains
    for idx, (qi, j) in enumerate(steps):
        sts = nxt
        if idx + 1 < len(steps):
            nxt = scores(steps[idx + 1])
        if j == qi:
            sts = [jnp.where(mask_t, st, NEG) for st in sts]
        pts, alphas = [], []
        for c in range(n_chains):
            top = jnp.max(sts[c], axis=0, keepdims=True)
            if j > 0:
                top = jnp.maximum(m[c], top)
                alphas.append(jnp.exp2(m[c] - top))
            m[c] = top
            pts.append(jnp.exp2(sts[c] - top).astype(BF16))
        vt = vt_tile(j)
        for c in range(n_chains):
            pv = _dot(vt, pts[c])
            acc[c] = pv if j == 0 else alphas[c] * acc[c] + pv
        if j == qi:
            finish(qi, [a[0:LANES, :] * (1.0 / a[LANES:LANES + 1, :]) for a in acc])


def _lam(lq1, lk1, lq2, lk2, lam_init):
    return (jnp.exp(jnp.sum(lq1[...] * lk1[...], axis=1, keepdims=True))
            - jnp.exp(jnp.sum(lq2[...] * lk2[...], axis=1, keepdims=True)) + lam_init)


def _head_split(q):
    lo = _lane_lo()
    zero = jnp.zeros_like(q)
    return [jnp.where(lo, q, zero), jnp.where(lo, zero, q)]


def _attn_a_kernel(q_ref, k_ref, v_ref, lq1, lk1, lq2, lk2, gsub_ref, sg_ref, o_ref, kb_ref, vt_ref,
                   *, tq, t, lam_init):
    kb_ref[...] = k_ref[0].astype(BF16)
    vt_ref[0:LANES, :] = v_ref[0].T.astype(BF16)
    vt_ref[LANES:V_ROWS, :] = _ones_rows(t)
    lam = _lam(lq1, lk1, lq2, lk2, lam_init)
    tile = lambda i: slice(i * tq, (i + 1) * tq)
    split = {}

    def q_tile(c, qi):
        if qi not in split:
            split[qi] = _head_split(q_ref[0, tile(qi), :])
        return split[qi][c]

    def finish(qi, outs):
        o = outs[0].T - lam * outs[1].T
        o = _rms(o, gsub_ref[...]) * (1.0 - lam_init)
        o_ref[0, tile(qi), :] = (o * sg_ref[0, tile(qi), :].astype(F32)).astype(BF16)

    _causal_attend_all(t // tq, 2, q_tile, lambda c, j: kb_ref[tile(j), :], lambda j: vt_ref[:, tile(j)],
                       _chunk_mask_t(tq), finish)


def _attn_a(q, k, v, lams, gsub, sg, lam_init):
    b, t, _ = q.shape
    tq = min(TQ, t)
    blk = pl.BlockSpec((1, t, LANES), lambda bi, h: (bi, 0, h))
    vec = lambda n: pl.BlockSpec((1, n), lambda bi, h: (0, 0))
    return pl.pallas_call(
        functools.partial(_attn_a_kernel, tq=tq, t=t, lam_init=lam_init),
        grid=(b, A_HEADS),
        in_specs=[blk, blk, blk, vec(A_QK), vec(A_QK), vec(A_QK), vec(A_QK), vec(A_V), blk],
        out_specs=blk,
        out_shape=jax.ShapeDtypeStruct((b, t, A_HEADS * A_V), BF16),
        scratch_shapes=[pltpu.VMEM((t, LANES), BF16), pltpu.VMEM((V_ROWS, t), BF16)],
        compiler_params=_params(("arbitrary", "arbitrary")),
        name="attn_a",
    )(q, k, v, *lams, gsub, sg)


def _attn_b_kernel(q_ref, k_ref, v_ref, bias_ref, sg_ref, o_ref, kb_ref, vt_ref, *, t):
    kb_ref[0:B_BAND, :] = jnp.zeros((B_BAND, LANES), BF16)
    kb_ref[B_BAND:B_BAND + t, :] = k_ref[0].astype(BF16)
    vt_ref[:, 0:B_BAND] = jnp.zeros((V_ROWS, B_BAND), BF16)
    vt_ref[0:LANES, B_BAND:B_BAND + t] = v_ref[0].T.astype(BF16)
    vt_ref[LANES:V_ROWS, B_BAND:B_BAND + t] = _ones_rows(t)
    key = lax.broadcasted_iota(jnp.int32, (WIN_B, 1), 0)
    row_lo = _row_lo()

    def tiles(q0s, first_valids):
        def scores(n):
            qa, qb = _head_split(q_ref[0, pl.ds(q0s[n], TQ_B), :])
            st = _dot_nt(kb_ref[pl.ds(q0s[n], WIN_B), :], jnp.concatenate([qa, qb], axis=0)) + bias_ref[0]
            if first_valids[n] > 0:
                st = jnp.where(key < first_valids[n], NEG, st)
            return st

        nxt = scores(0)
        for n in range(len(q0s)):
            st = nxt
            if n + 1 < len(q0s):
                nxt = scores(n + 1)
            pt = jnp.exp2(st - jnp.max(st, axis=0, keepdims=True)).astype(BF16)
            acc = _dot(vt_ref[:, pl.ds(q0s[n], WIN_B)], pt)
            inv = 1.0 / acc[LANES:LANES + 1, :]
            ot = jnp.where(row_lo, acc[0:LANES, 0:TQ_B] * inv[:, 0:TQ_B],
                           acc[0:LANES, TQ_B:2 * TQ_B] * inv[:, TQ_B:2 * TQ_B])
            rows = pl.ds(q0s[n], TQ_B)
            o_ref[0, rows, :] = (ot.T * sg_ref[0, rows, :].astype(F32)).astype(BF16)

    n_edge = min(B_BAND, t) // TQ_B
    tiles([qi * TQ_B for qi in range(n_edge)], [B_BAND - qi * TQ_B for qi in range(n_edge)])
    per_iter = 4

    def body(i, carry):
        q0 = pl.multiple_of(n_edge * TQ_B + i * per_iter * TQ_B, per_iter * TQ_B)
        tiles([q0 + n * TQ_B for n in range(per_iter)], [0] * per_iter)
        return carry

    lax.fori_loop(0, (t - n_edge * TQ_B) // (per_iter * TQ_B), body, 0)


def _attn_b(q, k, v, bias_t, sg):
    b, t, _ = q.shape
    assert t % (4 * TQ_B) == 0
    blk = pl.BlockSpec((1, t, LANES), lambda bi, h: (bi, 0, h))
    sgblk = pl.BlockSpec((1, t, LANES), lambda bi, h: (bi, 0, h + A_HEADS))
    bblk = pl.BlockSpec((1, WIN_B, 2 * TQ_B), lambda bi, h: (h, 0, 0))
    return pl.pallas_call(
        functools.partial(_attn_b_kernel, t=t),
        grid=(b, B_HEADS // 2),
        in_specs=[blk, blk, blk, bblk, sgblk],
        out_specs=blk,
        out_shape=jax.ShapeDtypeStruct((b, t, B_HEADS * B_HD), BF16),
        scratch_shapes=[pltpu.VMEM((B_BAND + t, LANES), BF16), pltpu.VMEM((V_ROWS, B_BAND + t), BF16)],
        compiler_params=_params(("arbitrary", "arbitrary")),
        name="attn_b",
    )(q, k, v, bias_t, sg)


def _attn_c_kernel(q_ref, clat_ref, kr_ref, wuk_ref, wuv_ref, sg_ref, o_ref, kc_ref, vt_ref, *, tq, t):
    cl = clat_ref[0].astype(BF16)
    kr = kr_ref[0]
    for hh in range(2):
        kc_ref[hh] = (_dot(cl, wuk_ref[hh]) + kr).astype(BF16)
    vt_ref[0:LANES, :] = _dot(cl, wuv_ref[0]).T.astype(BF16)
    vt_ref[LANES:V_ROWS, :] = _ones_rows(t)
    tile = lambda i: slice(i * tq, (i + 1) * tq)
    row_lo = _row_lo()

    def finish(qi, outs):
        o = jnp.where(row_lo, outs[0], outs[1]).T
        o_ref[0, tile(qi), :] = (o * sg_ref[0, tile(qi), :].astype(F32)).astype(BF16)

    _causal_attend_all(t // tq, 2, lambda c, qi: q_ref[0, tile(qi), c * LANES:(c + 1) * LANES],
                       lambda c, j: kc_ref[c, tile(j), :], lambda j: vt_ref[:, tile(j)],
                       _chunk_mask_t(tq), finish)


def _attn_c(qc, clat, kr, wuk, wuv, sg):
    b, t, _ = qc.shape
    tq = min(TQ, t)
    return pl.pallas_call(
        functools.partial(_attn_c_kernel, tq=tq, t=t),
        grid=(b, C_HEADS // 2),
        in_specs=[pl.BlockSpec((1, t, 2 * LANES), lambda bi, h: (bi, 0, h)),
                  pl.BlockSpec((1, t, C_KVL), lambda bi, h: (bi, 0, 0)),
                  pl.BlockSpec((1, t, LANES), lambda bi, h: (bi, 0, 0)),
                  pl.BlockSpec((2, C_KVL, LANES), lambda bi, h: (h, 0, 0)),
                  pl.BlockSpec((1, C_KVL, LANES), lambda bi, h: (h, 0, 0)),
                  pl.BlockSpec((1, t, LANES), lambda bi, h: (bi, 0, h))],
        out_specs=pl.BlockSpec((1, t, LANES), lambda bi, h: (bi, 0, h)),
        out_shape=jax.ShapeDtypeStruct((b, t, C_HEADS * C_V), BF16),
        scratch_shapes=[pltpu.VMEM((2, t, LANES), BF16), pltpu.VMEM((V_ROWS, t), BF16)],
        compiler_params=_params(("arbitrary", "arbitrary")),
        name="attn_c",
    )(qc, clat, kr, wuk, wuv, sg)


def _stick_units_t(qs, k, vt, negut, runs, tri_t):
    zts = [_dot_nt(k, q) for q in qs]
    sps = [_softplus(zt) for zt in zts]
    msps = sps if tri_t is None else [jnp.where(tri_t, sp, 0.0) for sp in sps]
    laters = [_dot(negut, msp.astype(BF16)) for msp in msps]
    outs = []
    for c in range(len(qs)):
        x = zts[c] - sps[c] + laters[c]
        if runs is not None:
            x = x + runs[c]
        a = jnp.exp(x)
        if tri_t is not None:
            a = jnp.where(tri_t, a, 0.0)
        outs.append((laters[c][0:1, :] - msps[c][0:1, :], a.astype(BF16)))
    return [(total, _dot(vt, a)) for total, a in outs]


def _attn_d_kernel(q_ref, k_ref, v_ref, negut_ref, sg_ref, o_ref, kb_ref, vt_ref, *, tq, t):
    qt = pl.program_id(2)

    @pl.when(qt == 0)
    def _():
        kb_ref[0:tq, :] = jnp.zeros((tq, LANES), BF16)
        vt_ref[:, 0:tq] = jnp.zeros((LANES, tq), BF16)
        kb_ref[tq:tq + t, :] = k_ref[0].astype(BF16)
        vt_ref[:, tq:tq + t] = v_ref[0].T.astype(BF16)

    qs = _head_split(q_ref[0])
    j = lax.broadcasted_iota(jnp.int32, (2 * tq, tq), 0)
    i = lax.broadcasted_iota(jnp.int32, (2 * tq, tq), 1)
    tri_t = j < i + tq
    win = pl.ds(pl.multiple_of(qt * tq, tq), 2 * tq)
    (run0, acc0), (run1, acc1) = _stick_units_t(qs, kb_ref[win, :], vt_ref[:, win], negut_ref[...], None, tri_t)

    def cond(carry):
        return jnp.logical_and(carry[0] < qt - 1, carry[1] > -104.0)

    def body(carry):
        n, _, run0, acc0, run1, acc1 = carry
        keys = pl.ds(pl.multiple_of((qt - 1 - n) * tq, tq), tq)
        (d0, c0), (d1, c1) = _stick_units_t(qs, kb_ref[keys, :], vt_ref[:, keys], negut_ref[0:tq, 0:tq],
                                            [run0, run1], None)
        run0, run1 = run0 + d0, run1 + d1
        return n + 1, jnp.maximum(jnp.max(run0), jnp.max(run1)), run0, acc0 + c0, run1, acc1 + c1

    init = (jnp.int32(0), jnp.maximum(jnp.max(run0), jnp.max(run1)), run0, acc0, run1, acc1)
    _, _, _, acc0, _, acc1 = lax.while_loop(cond, body, init)
    ot = jnp.where(_row_lo(), acc0, acc1)
    o_ref[0] = (ot.T * sg_ref[0].astype(F32)).astype(BF16)


def _attn_d(q, k, v, negut, sg):
    b, t, _ = q.shape
    tq = min(TQ_D, t)
    qblk = pl.BlockSpec((1, tq, LANES), lambda bi, h, qi: (bi, qi, h))
    sgblk = pl.BlockSpec((1, tq, LANES), lambda bi, h, qi: (bi, qi, h + C_HEADS // 2))
    kvblk = pl.BlockSpec((1, t, LANES), lambda bi, h, qi: (bi, 0, h))
    ublk = pl.BlockSpec((2 * tq, 2 * tq), lambda bi, h, qi: (0, 0))
    return pl.pallas_call(
        functools.partial(_attn_d_kernel, tq=tq, t=t),
        grid=(b, D_HEADS // 2, t // tq),
        in_specs=[qblk, kvblk, kvblk, ublk, sgblk],
        out_specs=qblk,
        out_shape=jax.ShapeDtypeStruct((b, t, D_HEADS * D_HD), BF16),
        scratch_shapes=[pltpu.VMEM((tq + t, LANES), BF16), pltpu.VMEM((LANES, tq + t), BF16)],
        compiler_params=_params(("arbitrary", "arbitrary", "arbitrary")),
        name="attn_d",
    )(q, k, v, negut, sg)


def _two_part_softmax(q, kc, kn, vc, vn, bc=None, bn=None):
    sc = _dot_nt(q, kc)
    sn = _dot_nt(q, kn)
    if bc is not None:
        sc = sc + bc
        sn = sn + bn
    m = jnp.maximum(jnp.max(sc, axis=1, keepdims=True), jnp.max(sn, axis=1, keepdims=True))
    pc = jnp.exp2(sc - m)
    pn = jnp.exp2(sn - m)
    l = jnp.sum(pc, axis=1, keepdims=True) + jnp.sum(pn, axis=1, keepdims=True)
    return (_dot(pc.astype(BF16), vc) + _dot(pn.astype(BF16), vn)) * (1.0 / l)


def _samp_a_kernel(q_ref, ck_ref, cv_ref, nk_ref, nv_ref, lq1, lk1, lq2, lk2, gsub_ref, sg_ref, o_ref,
                   *, lam_init):
    lam = _lam(lq1, lk1, lq2, lk2, lam_init)
    for h in range(A_HEADS):
        sl = slice(h * LANES, (h + 1) * LANES)
        kc, vc = ck_ref[0, :, sl].astype(BF16), cv_ref[0, :, sl].astype(BF16)
        kn, vn = nk_ref[0, :, sl].astype(BF16), nv_ref[0, :, sl].astype(BF16)
        q0, q1 = _head_split(q_ref[0, :, sl])
        o0 = _two_part_softmax(q0, kc, kn, vc, vn)
        o1 = _two_part_softmax(q1, kc, kn, vc, vn)
        o = _rms(o0 - lam * o1, gsub_ref[...]) * (1.0 - lam_init)
        o_ref[0, :, sl] = (o * sg_ref[0, :, sl].astype(F32)).astype(BF16)


def _samp_b_kernel(q_ref, ck_ref, cv_ref, nk_ref, nv_ref, bias_ref, sg_ref, o_ref, *, band, st):
    lo = _lane_lo()
    for hp in range(B_HEADS // 2):
        sl = slice(hp * LANES, (hp + 1) * LANES)
        kc, vc = ck_ref[0, :, sl].astype(BF16), cv_ref[0, :, sl].astype(BF16)
        kn, vn = nk_ref[0, :, sl].astype(BF16), nv_ref[0, :, sl].astype(BF16)
        outs = []
        for hh, qm in enumerate(_head_split(q_ref[0, :, sl])):
            bias = bias_ref[2 * hp + hh]
            outs.append(_two_part_softmax(qm, kc, kn, vc, vn, bias[:, 0:band], bias[:, band:band + st]))
        o = jnp.where(lo, outs[0], outs[1])
        o_ref[0, :, sl] = (o * sg_ref[0, :, sl].astype(F32)).astype(BF16)


def _samp_c_kernel(q_ref, clat_ref, krp_ref, nclat_ref, nkr_ref, wuk_ref, wuv_ref, sg_ref, o_ref):
    lo = _lane_lo()
    cl = clat_ref[0].astype(BF16)
    ncl = nclat_ref[0].astype(BF16)
    krc = krp_ref[0].astype(F32)
    krn = nkr_ref[0]
    for hp in range(C_HEADS // 2):
        sl = slice(hp * LANES, (hp + 1) * LANES)
        vc = _dot(cl, wuv_ref[hp]).astype(BF16)
        vn = _dot(ncl, wuv_ref[hp]).astype(BF16)
        outs = []
        for hh in range(2):
            h = 2 * hp + hh
            kc = (_dot(cl, wuk_ref[h]) + krc).astype(BF16)
            kn = (_dot(ncl, wuk_ref[h]) + krn).astype(BF16)
            outs.append(_two_part_softmax(q_ref[0, :, h * LANES:(h + 1) * LANES], kc, kn, vc, vn))
        o = jnp.where(lo, outs[0], outs[1])
        o_ref[0, :, sl] = (o * sg_ref[0, :, sl].astype(F32)).astype(BF16)


def _stick_unit(q, k, v, negu, run, tri):
    z = _dot_nt(q, k)
    sp = _softplus(z)
    msp = sp if tri is None else jnp.where(tri, sp, 0.0)
    later = _dot(msp.astype(BF16), negu)
    x = z - sp + later
    if run is not None:
        x = x + run
    a = jnp.exp(x)
    if tri is not None:
        a = jnp.where(tri, a, 0.0)
    total = later[:, 0:1] - msp[:, 0:1]
    return total, _dot(a.astype(BF16), v)


def _neg_suffix(n):
    r = lax.broadcasted_iota(jnp.int32, (n, n), 0)
    c = lax.broadcasted_iota(jnp.int32, (n, n), 1)
    return jnp.where(r > c, -1.0, 0.0).astype(BF16)


def _samp_d_kernel(q_ref, ck_ref, cv_ref, nk_ref, nv_ref, sg_ref, o_ref, *, past, st, tk):
    lo = _lane_lo()
    negu_new = _neg_suffix(st)
    r = lax.broadcasted_iota(jnp.int32, (st, st), 0)
    c = lax.broadcasted_iota(jnp.int32, (st, st), 1)
    tri = c < r
    negu = _neg_suffix(tk)
    n_chunks = past // tk
    for hp in range(D_HEADS // 2):
        sl = slice(hp * LANES, (hp + 1) * LANES)
        kn, vn = nk_ref[0, :, sl].astype(BF16), nv_ref[0, :, sl].astype(BF16)
        qa, qb = _head_split(q_ref[0, :, sl])
        run_a, acc_a = _stick_unit(qa, kn, vn, negu_new, None, tri)
        run_b, acc_b = _stick_unit(qb, kn, vn, negu_new, None, tri)

        def cond(carry):
            return jnp.logical_and(carry[0] < n_chunks, carry[1] > -104.0)

        def body(carry, qa=qa, qb=qb, sl=sl):
            n, _, run_a, acc_a, run_b, acc_b = carry
            rows = pl.ds(pl.multiple_of((n_chunks - 1 - n) * tk, tk), tk)
            kc, vc = ck_ref[0, rows, sl].astype(BF16), cv_ref[0, rows, sl].astype(BF16)
            da, ca = _stick_unit(qa, kc, vc, negu, run_a, None)
            db, cb = _stick_unit(qb, kc, vc, negu, run_b, None)
            run_a, run_b = run_a + da, run_b + db
            return n + 1, jnp.maximum(jnp.max(run_a), jnp.max(run_b)), run_a, acc_a + ca, run_b, acc_b + cb

        init = (jnp.int32(0), jnp.maximum(jnp.max(run_a), jnp.max(run_b)), run_a, acc_a, run_b, acc_b)
        _, _, _, acc_a, _, acc_b = lax.while_loop(cond, body, init)
        o = jnp.where(lo, acc_a, acc_b)
        o_ref[0, :, sl] = (o * sg_ref[0, :, sl].astype(F32)).astype(BF16)


def _samp_call(body, name, b, st, ins, specs):
    return pl.pallas_call(
        body,
        grid=(b,),
        in_specs=specs,
        out_specs=pl.BlockSpec((1, st, 512), lambda bi: (bi, 0, 0)),
        out_shape=jax.ShapeDtypeStruct((b, st, 512), BF16),
        compiler_params=_params(("arbitrary",)),
        name=name,
    )(*ins)


def _per_batch(shape, col=0):
    return pl.BlockSpec((1,) + tuple(shape[1:]), lambda bi: (bi,) + (0,) * (len(shape) - 2) + (col,))


def _whole(shape):
    return pl.BlockSpec(tuple(shape), lambda bi: (0,) * len(shape))


def _rope_tables(pos, rot, period, offset):
    half = rot // 2
    inv_freq = ROPE_THETA ** (-jnp.arange(half, dtype=F32) / half)
    ang = pos.astype(F32)[:, None] * inv_freq[None, :]
    cos, sin = jnp.cos(ang), jnp.sin(ang)
    n = pos.shape[0]
    c = jnp.ones((n, period), F32).at[:, offset:offset + half].set(cos).at[:, offset + half:offset + rot].set(cos)
    s1 = jnp.zeros((n, period), F32).at[:, offset:offset + half].set(-sin)
    s2 = jnp.zeros((n, period), F32).at[:, offset + half:offset + rot].set(sin)
    reps = LANES // period
    return tuple(jnp.tile(x, (1, reps)) for x in (c, s1, s2))


def kernel(x_prompt, x_sample, cache_a_k, cache_a_v, cache_b_k, cache_b_v, cache_c_latent, cache_c_krope,
           cache_d_k, cache_d_v, g_pre0, w_in0, lam_q1, lam_k1, lam_q2, lam_k2, g_sub_a, rel_bias_b, w_out0,
           g_post0, g_pre1, w_in1, g_cq, w_uq, g_ckv, w_uk, w_uv, w_out1, g_post1):
    b, t, _ = x_prompt.shape
    sb, st, _ = x_sample.shape
    past = cache_a_k.shape[1]
    band = cache_b_k.shape[1]
    assert band == B_BAND and t % TQ == 0 and past % TQ_D == 0
    ns = sb * st
    row = lambda x: x.reshape(1, -1)

    w0 = w_in0.astype(BF16)
    wo0 = w_out0.astype(BF16)
    wo1 = w_out1.astype(BF16)
    w1 = jnp.concatenate([w_in1[:, 0:1024], w_in1[:, 1056:3616], jnp.zeros((D_MODEL, 64), F32),
                          w_in1[:, 1024:1056], jnp.zeros((D_MODEL, 32), F32)], axis=1).astype(BF16)
    wuq = jnp.pad(w_uq.reshape(C_QL, C_HEADS, C_NOPE + C_ROPE), ((0, 0), (0, 0), (0, 32)))
    wuq = wuq.reshape(C_QL, C_HEADS * LANES).astype(BF16)
    wuk = jnp.pad(w_uk.transpose(1, 0, 2), ((0, 0), (0, 0), (0, LANES - C_NOPE))).astype(BF16)
    wuv = w_uv.reshape(C_KVL, C_HEADS // 2, 2 * C_V).transpose(1, 0, 2).astype(BF16)
    n_u = 2 * min(TQ_D, t)
    negut = -jnp.triu(jnp.ones((n_u, n_u), BF16), 1)
    lams = tuple(row(x) for x in (lam_q1, lam_k1, lam_q2, lam_k2))
    lam_init = 0.8 - 0.6 * math.exp(-0.3 * 0)

    pos_p = jnp.arange(t)
    pos_s = past + jnp.arange(st)
    tabs0_p = _rope_tables(pos_p, A_ROT, A_QK, 0)
    tabs0_s = tuple(jnp.tile(x, (sb, 1)) for x in _rope_tables(pos_s, A_ROT, A_QK, 0))
    tabs1_p = _rope_tables(pos_p, C_ROPE, LANES, C_NOPE)
    tabs1_s = tuple(jnp.tile(x, (sb, 1)) for x in _rope_tables(pos_s, C_ROPE, LANES, C_NOPE))

    bias_t, bias_s = _bias_tiles(rel_bias_b, st)

    aq, ak, av, bq, bk, bv, sg0 = _proj0(x_prompt, row(g_pre0), w0, tabs0_p, 256)
    oa = _attn_a(aq, ak, av, lams, row(g_sub_a), sg0, lam_init)
    ob = _attn_b(bq, bk, bv, bias_t, sg0)
    h1 = _outproj(oa.reshape(b * t, 512), ob.reshape(b * t, 512), wo0, row(g_post0),
                  x_prompt.reshape(b * t, D_MODEL), 512)

    xs = x_sample.reshape(1, ns, D_MODEL)
    s_out = _proj0(xs, row(g_pre0), w0, tabs0_s, ns)
    aq_s, ak_s, av_s, bq_s, bk_s, bv_s, sg0_s = (x.reshape(sb, st, -1) for x in s_out)
    vec = lambda n: _whole((1, n))
    oa_s = _samp_call(
        functools.partial(_samp_a_kernel, lam_init=lam_init), "samp_a", sb, st,
        (aq_s, cache_a_k.reshape(sb, past, 512), cache_a_v.reshape(sb, past, 512), ak_s, av_s,
         *lams, row(g_sub_a), sg0_s),
        [_per_batch((sb, st, 512)), _per_batch((sb, past, 512)), _per_batch((sb, past, 512)),
         _per_batch((sb, st, 512)), _per_batch((sb, st, 512)), vec(A_QK), vec(A_QK), vec(A_QK), vec(A_QK),
         vec(A_V), _per_batch((sb, st, 512))])
    ob_s = _samp_call(
        functools.partial(_samp_b_kernel, band=band, st=st), "samp_b", sb, st,
        (bq_s, cache_b_k.reshape(sb, band, 512), cache_b_v.reshape(sb, band, 512), bk_s, bv_s, bias_s, sg0_s),
        [_per_batch((sb, st, 512)), _per_batch((sb, band, 512)), _per_batch((sb, band, 512)),
         _per_batch((sb, st, 512)), _per_batch((sb, st, 512)),
         pl.BlockSpec((B_HEADS, st, WIN_B), lambda bi: (0, 0, 0)), _per_batch((sb, st, 512), col=1)])
    hs1 = _outproj(oa_s.reshape(ns, 512), ob_s.reshape(ns, 512), wo0, row(g_post0),
                   x_sample.reshape(ns, D_MODEL), ns)

    qc, clat, kr, dq, dk, dv, sg1 = _proj1(h1.reshape(b, t, D_MODEL), row(g_pre1), w1, row(g_cq), wuq,
                                           row(g_ckv), tabs1_p, 256)
    oc = _attn_c(qc, clat, kr, wuk, wuv, sg1)
    od = _attn_d(dq, dk, dv, negut, sg1)
    h2 = _outproj(oc.reshape(b * t, 512), od.reshape(b * t, 512), wo1, row(g_post1), h1, 512)

    s_out = _proj1(hs1.reshape(1, ns, D_MODEL), row(g_pre1), w1, row(g_cq), wuq, row(g_ckv), tabs1_s, ns)
    qc_s, clat_s, kr_s, dq_s, dk_s, dv_s, sg1_s = (x.reshape(sb, st, -1) for x in s_out)
    krp = jnp.pad(cache_c_krope, ((0, 0), (0, 0), (C_NOPE, LANES - C_NOPE - C_ROPE))).astype(BF16)
    oc_s = _samp_call(
        _samp_c_kernel, "samp_c", sb, st,
        (qc_s, cache_c_latent, krp, clat_s, kr_s, wuk, wuv, sg1_s),
        [_per_batch((sb, st, 1024)), _per_batch((sb, past, C_KVL)), _per_batch((sb, past, LANES)),
         _per_batch((sb, st, C_KVL)), _per_batch((sb, st, LANES)), _whole(wuk.shape), _whole(wuv.shape),
         _per_batch((sb, st, 512))])
    od_s = _samp_call(
        functools.partial(_samp_d_kernel, past=past, st=st, tk=TQ_D), "samp_d", sb, st,
        (dq_s, cache_d_k.reshape(sb, past, 512), cache_d_v.reshape(sb, past, 512), dk_s, dv_s, sg1_s),
        [_per_batch((sb, st, 512)), _per_batch((sb, past, 512)), _per_batch((sb, past, 512)),
         _per_batch((sb, st, 512)), _per_batch((sb, st, 512)), _per_batch((sb, st, 512), col=1)])
    hs2 = _outproj(oc_s.reshape(ns, 512), od_s.reshape(ns, 512), wo1, row(g_post1), hs1, ns)

    b_rows = min(B_BAND, t)
    return (h2.reshape(b, t, D_MODEL), hs2.reshape(sb, st, D_MODEL),
            ak.reshape(b, t, A_HEADS, 2, A_QK), av.reshape(b, t, A_HEADS, A_V),
            bk[:, t - b_rows:].reshape(b, b_rows, B_HEADS, B_HD), bv[:, t - b_rows:].reshape(b, b_rows, B_HEADS, B_HD),
            clat, kr[:, :, C_NOPE:C_NOPE + C_ROPE],
            dk.reshape(b, t, D_HEADS, D_HD), dv.reshape(b, t, D_HEADS, D_HD),
            ak_s.reshape(sb, st, A_HEADS, 2, A_QK), av_s.reshape(sb, st, A_HEADS, A_V),
            bk_s.reshape(sb, st, B_HEADS, B_HD), bv_s.reshape(sb, st, B_HEADS, B_HD),
            clat_s, kr_s[:, :, C_NOPE:C_NOPE + C_ROPE],
            dk_s.reshape(sb, st, D_HEADS, D_HD), dv_s.reshape(sb, st, D_HEADS, D_HD))
```

```python
import functools
import math

import jax
import jax.numpy as jnp
from jax import lax
from jax.experimental import pallas as pl
from jax.experimental.pallas import tpu as pltpu

F32 = jnp.float32
BF16 = jnp.bfloat16

D_MODEL = 1024
CHUNK = 64
ROPE_THETA = 500000.0
NORM_EPS = 1e-6
A_HEADS, A_QK, A_V, A_ROT = 4, 64, 128, 16
B_HEADS, B_HD, B_LEFT, B_CLIP = 8, 64, 8, 128
C_HEADS, C_NOPE, C_ROPE, C_V, C_QL, C_KVL = 8, 64, 32, 64, 768, 256
D_HEADS, D_HD = 8, 64
B_BAND = B_LEFT * CHUNK
LANES = 128
NEG = -1e30
VMEM_LIMIT = 56 * 1024 * 1024

TQ = 512
TQ_D = 256
TQ_B = 128
WIN_B = B_BAND + TQ_B
LOG2E = 1.4426950408889634
V_ROWS = 144


def _dot(a, b):
    return jnp.dot(a, b, preferred_element_type=F32)


def _dot_nt(a, b):
    return lax.dot_general(a, b, (((1,), (1,)), ((), ())), preferred_element_type=F32)


def _rms(x, g):
    ms = jnp.mean(x * x, axis=-1, keepdims=True)
    return x * lax.rsqrt(ms + NORM_EPS) * g


def _silu(x):
    return x * (1.0 / (1.0 + jnp.exp(-x)))


def _softplus(z):
    return jnp.maximum(z, 0.0) + jnp.log(1.0 + jnp.exp(-jnp.abs(z)))


def _rope(z, c, s1, s2, shift):
    return z * c + pltpu.roll(z, LANES - shift, 1) * s1 + pltpu.roll(z, shift, 1) * s2


def _lane_lo():
    return lax.broadcasted_iota(jnp.int32, (1, LANES), 1) < 64


def _row_lo():
    return lax.broadcasted_iota(jnp.int32, (LANES, 1), 0) < 64


def _params(sem):
    return pltpu.CompilerParams(dimension_semantics=sem, vmem_limit_bytes=VMEM_LIMIT)


def _proj0_kernel(x_ref, g_ref, w_ref, rc_ref, rs1_ref, rs2_ref,
                  aq_ref, ak_ref, av_ref, bq_ref, bk_ref, bv_ref, sg_ref):
    u = _rms(x_ref[0], g_ref[...]).astype(BF16)
    c, s1, s2 = rc_ref[...], rs1_ref[...], rs2_ref[...]
    zq = _dot(u, w_ref[:, 0:512])
    zk = _dot(u, w_ref[:, 512:1024])
    for v in range(4):
        sl = slice(v * LANES, (v + 1) * LANES)
        aq_ref[0, :, sl] = (_rope(zq[:, sl], c, s1, s2, A_ROT // 2) * (A_QK ** -0.5 * LOG2E)).astype(BF16)
        ak_ref[0, :, sl] = _rope(zk[:, sl], c, s1, s2, A_ROT // 2)
    av_ref[0] = _dot(u, w_ref[:, 1024:1536])
    bq_ref[0] = (_dot(u, w_ref[:, 1536:2048]) * (B_HD ** -0.5 * LOG2E)).astype(BF16)
    bk_ref[0] = _dot(u, w_ref[:, 2048:2560])
    bv_ref[0] = _dot(u, w_ref[:, 2560:3072])
    sg_ref[0] = _silu(_dot(u, w_ref[:, 3072:4096])).astype(BF16)


def _proj0(x, g, w, tabs, tm):
    b, t, _ = x.shape
    tok = lambda n: pl.BlockSpec((1, tm, n), lambda ti, bi: (bi, ti, 0))
    const = lambda shape: pl.BlockSpec(shape, lambda ti, bi: (0,) * len(shape))
    tab = pl.BlockSpec((tm, LANES), lambda ti, bi: (ti, 0))
    widths = (512, 512, 512, 512, 512, 512, 1024)
    dts = (BF16, F32, F32, BF16, F32, F32, BF16)
    return pl.pallas_call(
        _proj0_kernel,
        grid=(t // tm, b),
        in_specs=[tok(D_MODEL), const((1, D_MODEL)), const(w.shape), tab, tab, tab],
        out_specs=[tok(n) for n in widths],
        out_shape=[jax.ShapeDtypeStruct((b, t, n), d) for n, d in zip(widths, dts)],
        compiler_params=_params(("arbitrary", "arbitrary")),
        name="proj0",
    )(x, g, w, *tabs)


def _proj1_kernel(x_ref, g_ref, w_ref, gcq_ref, wuq_ref, gckv_ref, rc_ref, rs1_ref, rs2_ref,
                  qc_ref, clat_ref, kr_ref, dq_ref, dk_ref, dv_ref, sg_ref):
    u = _rms(x_ref[0], g_ref[...]).astype(BF16)
    c, s1, s2 = rc_ref[...], rs1_ref[...], rs2_ref[...]
    cq = _rms(_dot(u, w_ref[:, 0:768]), gcq_ref[...]).astype(BF16)
    qc = _dot(cq, wuq_ref[...])
    for h in range(C_HEADS):
        sl = slice(h * LANES, (h + 1) * LANES)
        qc_ref[0, :, sl] = (_rope(qc[:, sl], c, s1, s2, C_ROPE // 2)
                            * ((C_NOPE + C_ROPE) ** -0.5 * LOG2E)).astype(BF16)
    clat_ref[0] = _rms(_dot(u, w_ref[:, 768:1024]), gckv_ref[...])
    dq_ref[0] = (_dot(u, w_ref[:, 1024:1536]) * (D_HD ** -0.5)).astype(BF16)
    dk_ref[0] = _dot(u, w_ref[:, 1536:2048])
    dv_ref[0] = _dot(u, w_ref[:, 2048:2560])
    sg_ref[0] = _silu(_dot(u, w_ref[:, 2560:3584])).astype(BF16)
    kr_ref[0] = _rope(_dot(u, w_ref[:, 3584:3712]), c, s1, s2, C_ROPE // 2)


def _proj1(x, g, w, gcq, wuq, gckv, tabs, tm):
    b, t, _ = x.shape
    tok = lambda n: pl.BlockSpec((1, tm, n), lambda ti, bi: (bi, ti, 0))
    const = lambda shape: pl.BlockSpec(shape, lambda ti, bi: (0,) * len(shape))
    tab = pl.BlockSpec((tm, LANES), lambda ti, bi: (ti, 0))
    widths = (1024, 256, 128, 512, 512, 512, 1024)
    dts = (BF16, F32, F32, BF16, F32, F32, BF16)
    return pl.pallas_call(
        _proj1_kernel,
        grid=(t // tm, b),
        in_specs=[tok(D_MODEL), const((1, D_MODEL)), const(w.shape), const((1, C_QL)),
                  const(wuq.shape), const((1, C_KVL)), tab, tab, tab],
        out_specs=[tok(n) for n in widths],
        out_shape=[jax.ShapeDtypeStruct((b, t, n), d) for n, d in zip(widths, dts)],
        compiler_params=_params(("arbitrary", "arbitrary")),
        name="proj1",
    )(x, g, w, gcq, wuq, gckv, *tabs)


def _outproj_kernel(o1_ref, o2_ref, w_ref, g_ref, h_ref, out_ref):
    y = _dot(o1_ref[...], w_ref[0:512, :]) + _dot(o2_ref[...], w_ref[512:1024, :])
    out_ref[...] = h_ref[...] + _rms(y, g_ref[...])


def _outproj(o1, o2, w, g, h, tm):
    n = h.shape[0]
    row = lambda c: pl.BlockSpec((tm, c), lambda i: (i, 0))
    const = lambda shape: pl.BlockSpec(shape, lambda i: (0,) * len(shape))
    return pl.pallas_call(
        _outproj_kernel,
        grid=(n // tm,),
        in_specs=[row(512), row(512), const(w.shape), const((1, D_MODEL)), row(D_MODEL)],
        out_specs=row(D_MODEL),
        out_shape=jax.ShapeDtypeStruct((n, D_MODEL), F32),
        compiler_params=_params(("arbitrary",)),
        name="outproj",
    )(o1, o2, w, g, h)


def _bias_lookup(rb_ref, h, idx):
    def body(r, acc):
        return jnp.where(idx == r, rb_ref[h, r], acc)

    return lax.fori_loop(0, 2 * B_CLIP + 1, body, jnp.zeros(idx.shape, F32)) * LOG2E


def _bias_kernel(rb_ref, out_t_ref, out_s_ref, *, st):
    h = pl.program_id(0)
    j = lax.broadcasted_iota(jnp.int32, (WIN_B, TQ_B), 0)
    i = lax.broadcasted_iota(jnp.int32, (WIN_B, TQ_B), 1)
    cj = lax.shift_right_logical(j, 6)
    ci = lax.shift_right_logical(i, 6)
    band = jnp.logical_and(cj >= ci, cj <= ci + B_LEFT)
    idx = jnp.clip(B_BAND + i - j, -B_CLIP, B_CLIP) + B_CLIP
    out_t_ref[0] = jnp.where(band, _bias_lookup(rb_ref, h, idx), NEG)
    i = lax.broadcasted_iota(jnp.int32, (st, WIN_B), 0)
    j = lax.broadcasted_iota(jnp.int32, (st, WIN_B), 1)
    out_s_ref[0] = _bias_lookup(rb_ref, h, jnp.clip(B_BAND + i - j, -B_CLIP, B_CLIP) + B_CLIP)


def _bias_tiles(rel_bias, st):
    return pl.pallas_call(
        functools.partial(_bias_kernel, st=st),
        grid=(B_HEADS,),
        in_specs=[pl.BlockSpec(memory_space=pltpu.SMEM)],
        out_specs=[pl.BlockSpec((1, WIN_B, TQ_B), lambda h: (h // 2, 0, h % 2)),
                   pl.BlockSpec((1, st, WIN_B), lambda h: (h, 0, 0))],
        out_shape=[jax.ShapeDtypeStruct((B_HEADS // 2, WIN_B, 2 * TQ_B), F32),
                   jax.ShapeDtypeStruct((B_HEADS, st, WIN_B), F32)],
        compiler_params=_params(("arbitrary",)),
        name="bias_tile",
    )(rel_bias)


def _ones_rows(n):
    r = lax.broadcasted_iota(jnp.int32, (V_ROWS - LANES, n), 0)
    return jnp.where(r == 0, 1.0, 0.0).astype(BF16)


def _causal_attend_all(n_tiles, n_chains, tq, q_tile, k_rows, vt_cols, finish):
    half = tq // 2
    steps = []
    for qi in range(n_tiles):
        steps += [(qi, j * tq, tq, 0, False) for j in range(qi)]
        steps += [(qi, qi * tq, half, 0, True), (qi, qi * tq + half, half, half, True)]
    qcache = {}

    def scores(step):
        qi, k0, nk, q0, _ = step
        for c in range(n_chains):
            if (c, qi) not in qcache:
                qcache[(c, qi)] = q_tile(c, qi)
        return [_dot_nt(k_rows(c, k0, nk), qcache[(c, qi)][q0:tq, :]) for c in range(n_chains)]

    def chunk_mask(step):
        _, k0, nk, q0, _ = step
        j = lax.broadcasted_iota(jnp.int32, (nk, tq - q0), 0) + (k0 % tq)
        i = lax.broadcasted_iota(jnp.int32, (nk, tq - q0), 1) + q0
        return lax.shift_right_logical(j, 6) <= lax.shift_right_logical(i, 6)

    nxt = scores(steps[0])
    m, acc = [None] * n_chains, [None] * n_chains
    for idx, step in enumerate(steps):
        qi, k0, nk, q0, masked = step
        sts = nxt
        if idx + 1 < len(steps):
            nxt = scores(steps[idx + 1])
        if masked:
            msk = chunk_mask(step)
            sts = [jnp.where(msk, st, NEG) for st in sts]
        first = idx == 0 or steps[idx - 1][0] != qi
        pts, alphas = [], []
        for c in range(n_chains):
            top = jnp.max(sts[c], axis=0, keepdims=True)
            if not first:
                old = m[c][:, q0:tq]
                top = jnp.maximum(old, top)
                alphas.append(jnp.exp2(old - top))
                m[c] = top if q0 == 0 else jnp.concatenate([m[c][:, 0:q0], top], axis=1)
            else:
                m[c] = top
            pts.append(jnp.exp2(sts[c] - top).astype(BF16))
        vt = vt_cols(k0, nk)
        for c in range(n_chains):
            pv = _dot(vt, pts[c])
            if first:
                acc[c] = pv
            elif q0 == 0:
                acc[c] = alphas[c] * acc[c] + pv
            else:
                acc[c] = jnp.concatenate([acc[c][:, 0:q0], alphas[c] * acc[c][:, q0:tq] + pv], axis=1)
        if idx + 1 == len(steps) or steps[idx + 1][0] != qi:
            finish(qi, [a[0:LANES, :] * (1.0 / a[LANES:LANES + 1, :]) for a in acc])


def _lam(lq1, lk1, lq2, lk2, lam_init):
    return (jnp.exp(jnp.sum(lq1[...] * lk1[...], axis=1, keepdims=True))
            - jnp.exp(jnp.sum(lq2[...] * lk2[...], axis=1, keepdims=True)) + lam_init)


def _head_split(q):
    lo = _lane_lo()
    zero = jnp.zeros_like(q)
    return [jnp.where(lo, q, zero), jnp.where(lo, zero, q)]


def _attn_a_kernel(q_ref, k_ref, v_ref, lq1, lk1, lq2, lk2, gsub_ref, sg_ref, o_ref, kb_ref, vt_ref,
                   *, tq, t, lam_init):
    kb_ref[...] = k_ref[0].astype(BF16)
    vt_ref[0:LANES, :] = v_ref[0].T.astype(BF16)
    vt_ref[LANES:V_ROWS, :] = _ones_rows(t)
    lam = _lam(lq1, lk1, lq2, lk2, lam_init)
    tile = lambda i: slice(i * tq, (i + 1) * tq)
    split = {}

    def q_tile(c, qi):
        if qi not in split:
            split[qi] = _head_split(q_ref[0, tile(qi), :])
        return split[qi][c]

    def finish(qi, outs):
        o = outs[0].T - lam * outs[1].T
        o = _rms(o, gsub_ref[...]) * (1.0 - lam_init)
        o_ref[0, tile(qi), :] = (o * sg_ref[0, tile(qi), :].astype(F32)).astype(BF16)

    _causal_attend_all(t // tq, 2, tq, q_tile, lambda c, k0, n: kb_ref[k0:k0 + n, :],
                       lambda k0, n: vt_ref[:, k0:k0 + n], finish)


def _attn_a(q, k, v, lams, gsub, sg, lam_init):
    b, t, _ = q.shape
    tq = min(TQ, t)
    blk = pl.BlockSpec((1, t, LANES), lambda bi, h: (bi, 0, h))
    vec = lambda n: pl.BlockSpec((1, n), lambda bi, h: (0, 0))
    return pl.pallas_call(
        functools.partial(_attn_a_kernel, tq=tq, t=t, lam_init=lam_init),
        grid=(b, A_HEADS),
        in_specs=[blk, blk, blk, vec(A_QK), vec(A_QK), vec(A_QK), vec(A_QK), vec(A_V), blk],
        out_specs=blk,
        out_shape=jax.ShapeDtypeStruct((b, t, A_HEADS * A_V), BF16),
        scratch_shapes=[pltpu.VMEM((t, LANES), BF16), pltpu.VMEM((V_ROWS, t), BF16)],
        compiler_params=_params(("arbitrary", "arbitrary")),
        name="attn_a",
    )(q, k, v, *lams, gsub, sg)


def _attn_b_kernel(q_ref, k_ref, v_ref, bias_ref, sg_ref, o_ref, kb_ref, vt_ref, *, t):
    kb_ref[0:B_BAND, :] = jnp.zeros((B_BAND, LANES), BF16)
    kb_ref[B_BAND:B_BAND + t, :] = k_ref[0].astype(BF16)
    vt_ref[:, 0:B_BAND] = jnp.zeros((V_ROWS, B_BAND), BF16)
    vt_ref[0:LANES, B_BAND:B_BAND + t] = v_ref[0].T.astype(BF16)
    vt_ref[LANES:V_ROWS, B_BAND:B_BAND + t] = _ones_rows(t)
    key = lax.broadcasted_iota(jnp.int32, (WIN_B, 1), 0)
    row_lo = _row_lo()

    def tiles(q0s, first_valids):
        def scores(n):
            qa, qb = _head_split(q_ref[0, pl.ds(q0s[n], TQ_B), :])
            st = _dot_nt(kb_ref[pl.ds(q0s[n], WIN_B), :], jnp.concatenate([qa, qb], axis=0)) + bias_ref[0]
            if first_valids[n] > 0:
                st = jnp.where(key < first_valids[n], NEG, st)
            return st

        nxt = scores(0)
        for n in range(len(q0s)):
            st = nxt
            if n + 1 < len(q0s):
                nxt = scores(n + 1)
            pt = jnp.exp2(st - jnp.max(st, axis=0, keepdims=True)).astype(BF16)
            acc = _dot(vt_ref[:, pl.ds(q0s[n], WIN_B)], pt)
            inv = 1.0 / acc[LANES:LANES + 1, :]
            ot = jnp.where(row_lo, acc[0:LANES, 0:TQ_B] * inv[:, 0:TQ_B],
                           acc[0:LANES, TQ_B:2 * TQ_B] * inv[:, TQ_B:2 * TQ_B])
            rows = pl.ds(q0s[n], TQ_B)
            o_ref[0, rows, :] = (ot.T * sg_ref[0, rows, :].astype(F32)).astype(BF16)

    n_edge = min(B_BAND, t) // TQ_B
    tiles([qi * TQ_B for qi in range(n_edge)], [B_BAND - qi * TQ_B for qi in range(n_edge)])
    per_iter = 4

    def body(i, carry):
        q0 = pl.multiple_of(n_edge * TQ_B + i * per_iter * TQ_B, per_iter * TQ_B)
        tiles([q0 + n * TQ_B for n in range(per_iter)], [0] * per_iter)
        return carry

    lax.fori_loop(0, (t - n_edge * TQ_B) // (per_iter * TQ_B), body, 0)


def _attn_b(q, k, v, bias_t, sg):
    b, t, _ = q.shape
    assert t % (4 * TQ_B) == 0
    blk = pl.BlockSpec((1, t, LANES), lambda bi, h: (bi, 0, h))
    sgblk = pl.BlockSpec((1, t, LANES), lambda bi, h: (bi, 0, h + A_HEADS))
    bblk = pl.BlockSpec((1, WIN_B, 2 * TQ_B), lambda bi, h: (h, 0, 0))
    return pl.pallas_call(
        functools.partial(_attn_b_kernel, t=t),
        grid=(b, B_HEADS // 2),
        in_specs=[blk, blk, blk, bblk, sgblk],
        out_specs=blk,
        out_shape=jax.ShapeDtypeStruct((b, t, B_HEADS * B_HD), BF16),
        scratch_shapes=[pltpu.VMEM((B_BAND + t, LANES), BF16), pltpu.VMEM((V_ROWS, B_BAND + t), BF16)],
        compiler_params=_params(("arbitrary", "arbitrary")),
        name="attn_b",
    )(q, k, v, bias_t, sg)


def _attn_c_kernel(q_ref, clat_ref, kr_ref, wuk_ref, wuv_ref, sg_ref, o_ref, kc_ref, vt_ref, *, tq, t):
    cl = clat_ref[0].astype(BF16)
    kr = kr_ref[0]
    for hh in range(2):
        kc_ref[hh] = (_dot(cl, wuk_ref[hh]) + kr).astype(BF16)
    vt_ref[0:LANES, :] = _dot(cl, wuv_ref[0]).T.astype(BF16)
    vt_ref[LANES:V_ROWS, :] = _ones_rows(t)
    tile = lambda i: slice(i * tq, (i + 1) * tq)
    row_lo = _row_lo()

    def finish(qi, outs):
        o = jnp.where(row_lo, outs[0], outs[1]).T
        o_ref[0, tile(qi), :] = (o * sg_ref[0, tile(qi), :].astype(F32)).astype(BF16)

    _causal_attend_all(t // tq, 2, tq, lambda c, qi: q_ref[0, tile(qi), c * LANES:(c + 1) * LANES],
                       lambda c, k0, n: kc_ref[c, k0:k0 + n, :], lambda k0, n: vt_ref[:, k0:k0 + n], finish)


def _attn_c(qc, clat, kr, wuk, wuv, sg):
    b, t, _ = qc.shape
    tq = min(TQ, t)
    return pl.pallas_call(
        functools.partial(_attn_c_kernel, tq=tq, t=t),
        grid=(b, C_HEADS // 2),
        in_specs=[pl.BlockSpec((1, t, 2 * LANES), lambda bi, h: (bi, 0, h)),
                  pl.BlockSpec((1, t, C_KVL), lambda bi, h: (bi, 0, 0)),
                  pl.BlockSpec((1, t, LANES), lambda bi, h: (bi, 0, 0)),
                  pl.BlockSpec((2, C_KVL, LANES), lambda bi, h: (h, 0, 0)),
                  pl.BlockSpec((1, C_KVL, LANES), lambda bi, h: (h, 0, 0)),
                  pl.BlockSpec((1, t, LANES), lambda bi, h: (bi, 0, h))],
        out_specs=pl.BlockSpec((1, t, LANES), lambda bi, h: (bi, 0, h)),
        out_shape=jax.ShapeDtypeStruct((b, t, C_HEADS * C_V), BF16),
        scratch_shapes=[pltpu.VMEM((2, t, LANES), BF16), pltpu.VMEM((V_ROWS, t), BF16)],
        compiler_params=_params(("arbitrary", "arbitrary")),
        name="attn_c",
    )(qc, clat, kr, wuk, wuv, sg)


def _stick_units_t(qs, ks, vts, negut, runs, tri_t):
    n = len(qs)
    zts = [_dot_nt(ks[c], qs[c]) for c in range(n)]
    sps = [_softplus(zt) for zt in zts]
    msps = sps if tri_t is None else [jnp.where(tri_t, sp, 0.0) for sp in sps]
    laters = [_dot(negut, msp.astype(BF16)) for msp in msps]
    outs = []
    for c in range(n):
        x = zts[c] - sps[c] + laters[c]
        if runs is not None:
            x = x + runs[c]
        a = jnp.exp(x)
        if tri_t is not None:
            a = jnp.where(tri_t, a, 0.0)
        outs.append((laters[c][0:1, :] - msps[c][0:1, :], a.astype(BF16)))
    return [(outs[c][0], _dot(vts[c], outs[c][1])) for c in range(n)]


def _attn_d_kernel(q_ref, k_ref, v_ref, negut_ref, sg_ref, o_ref, kb_ref, vt_ref, *, tk, t):
    qt = pl.program_id(2)

    @pl.when(qt == 0)
    def _():
        kb_ref[0:tk, :] = jnp.zeros((tk, LANES), BF16)
        vt_ref[:, 0:tk] = jnp.zeros((LANES, tk), BF16)
        kb_ref[tk:tk + t, :] = k_ref[0].astype(BF16)
        vt_ref[:, tk:tk + t] = v_ref[0].T.astype(BF16)

    qs = _head_split(q_ref[0, 0:tk, :]) + _head_split(q_ref[0, tk:2 * tk, :])
    half_of = (0, 0, 1, 1)
    j = lax.broadcasted_iota(jnp.int32, (2 * tk, tk), 0)
    i = lax.broadcasted_iota(jnp.int32, (2 * tk, tk), 1)
    tri_t = j < i + tk
    wins = [pl.ds(pl.multiple_of((2 * qt + h) * tk, tk), 2 * tk) for h in half_of]
    first = _stick_units_t(qs, [kb_ref[w, :] for w in wins], [vt_ref[:, w] for w in wins], negut_ref[...],
                           None, tri_t)
    runs = [f[0] for f in first]
    accs = [f[1] for f in first]

    def top(rs):
        return functools.reduce(jnp.maximum, [jnp.max(r) for r in rs])

    def cond(carry):
        return jnp.logical_and(carry[0] < 2 * qt, carry[1] > -104.0)

    def body(carry):
        n, _, runs, accs = carry
        tiles = [pl.ds(pl.multiple_of((2 * qt + h - 1 - n) * tk, tk), tk) for h in half_of]
        res = _stick_units_t(qs, [kb_ref[w, :] for w in tiles], [vt_ref[:, w] for w in tiles],
                             negut_ref[0:tk, 0:tk], runs, None)
        runs = [r + d for r, (d, _) in zip(runs, res)]
        accs = [a + c for a, (_, c) in zip(accs, res)]
        return n + 1, top(runs), runs, accs

    _, _, _, accs = lax.while_loop(cond, body, (jnp.int32(0), top(runs), runs, accs))
    row_lo = _row_lo()
    for h in range(2):
        ot = jnp.where(row_lo, accs[2 * h], accs[2 * h + 1])
        rows = slice(h * tk, (h + 1) * tk)
        o_ref[0, rows, :] = (ot.T * sg_ref[0, rows, :].astype(F32)).astype(BF16)


def _attn_d(q, k, v, negut, sg):
    b, t, _ = q.shape
    tk = min(TQ_D, t // 2)
    qblk = pl.BlockSpec((1, 2 * tk, LANES), lambda bi, h, qi: (bi, qi, h))
    sgblk = pl.BlockSpec((1, 2 * tk, LANES), lambda bi, h, qi: (bi, qi, h + C_HEADS // 2))
    kvblk = pl.BlockSpec((1, t, LANES), lambda bi, h, qi: (bi, 0, h))
    ublk = pl.BlockSpec((2 * tk, 2 * tk), lambda bi, h, qi: (0, 0))
    return pl.pallas_call(
        functools.partial(_attn_d_kernel, tk=tk, t=t),
        grid=(b, D_HEADS // 2, t // (2 * tk)),
        in_specs=[qblk, kvblk, kvblk, ublk, sgblk],
        out_specs=qblk,
        out_shape=jax.ShapeDtypeStruct((b, t, D_HEADS * D_HD), BF16),
        scratch_shapes=[pltpu.VMEM((tk + t, LANES), BF16), pltpu.VMEM((LANES, tk + t), BF16)],
        compiler_params=_params(("arbitrary", "arbitrary", "arbitrary")),
        name="attn_d",
    )(q, k, v, negut, sg)


def _two_part_softmax(q, kc, kn, vc, vn, bc=None, bn=None):
    sc = _dot_nt(q, kc)
    sn = _dot_nt(q, kn)
    if bc is not None:
        sc = sc + bc
        sn = sn + bn
    m = jnp.maximum(jnp.max(sc, axis=1, keepdims=True), jnp.max(sn, axis=1, keepdims=True))
    pc = jnp.exp2(sc - m)
    pn = jnp.exp2(sn - m)
    l = jnp.sum(pc, axis=1, keepdims=True) + jnp.sum(pn, axis=1, keepdims=True)
    return (_dot(pc.astype(BF16), vc) + _dot(pn.astype(BF16), vn)) * (1.0 / l)


def _samp_a_kernel(q_ref, ck_ref, cv_ref, nk_ref, nv_ref, lq1, lk1, lq2, lk2, gsub_ref, sg_ref, o_ref,
                   *, lam_init):
    lam = _lam(lq1, lk1, lq2, lk2, lam_init)
    for h in range(A_HEADS):
        sl = slice(h * LANES, (h + 1) * LANES)
        kc, vc = ck_ref[0, :, sl].astype(BF16), cv_ref[0, :, sl].astype(BF16)
        kn, vn = nk_ref[0, :, sl].astype(BF16), nv_ref[0, :, sl].astype(BF16)
        q0, q1 = _head_split(q_ref[0, :, sl])
        o0 = _two_part_softmax(q0, kc, kn, vc, vn)
        o1 = _two_part_softmax(q1, kc, kn, vc, vn)
        o = _rms(o0 - lam * o1, gsub_ref[...]) * (1.0 - lam_init)
        o_ref[0, :, sl] = (o * sg_ref[0, :, sl].astype(F32)).astype(BF16)


def _samp_b_kernel(q_ref, ck_ref, cv_ref, nk_ref, nv_ref, bias_ref, sg_ref, o_ref, *, band, st):
    lo = _lane_lo()
    for hp in range(B_HEADS // 2):
        sl = slice(hp * LANES, (hp + 1) * LANES)
        kc, vc = ck_ref[0, :, sl].astype(BF16), cv_ref[0, :, sl].astype(BF16)
        kn, vn = nk_ref[0, :, sl].astype(BF16), nv_ref[0, :, sl].astype(BF16)
        outs = []
        for hh, qm in enumerate(_head_split(q_ref[0, :, sl])):
            bias = bias_ref[2 * hp + hh]
            outs.append(_two_part_softmax(qm, kc, kn, vc, vn, bias[:, 0:band], bias[:, band:band + st]))
        o = jnp.where(lo, outs[0], outs[1])
        o_ref[0, :, sl] = (o * sg_ref[0, :, sl].astype(F32)).astype(BF16)


def _samp_c_kernel(q_ref, clat_ref, krp_ref, nclat_ref, nkr_ref, wukt_ref, wuv_ref, sg_ref, o_ref, *, st):
    lo = _lane_lo()
    cl = clat_ref[0].astype(BF16)
    ncl = nclat_ref[0].astype(BF16)
    krc = krp_ref[0]
    krn = nkr_ref[0].astype(BF16)
    qh = [q_ref[0, :, h * LANES:(h + 1) * LANES] for h in range(C_HEADS)]
    q_all = jnp.concatenate(qh, axis=0)
    q_lat = jnp.concatenate([_dot(qh[h], wukt_ref[h]) for h in range(C_HEADS)], axis=0).astype(BF16)
    sc = _dot_nt(q_lat, cl) + _dot_nt(q_all, krc)
    sn = _dot_nt(q_lat, ncl) + _dot_nt(q_all, krn)
    m = jnp.maximum(jnp.max(sc, axis=1, keepdims=True), jnp.max(sn, axis=1, keepdims=True))
    pc = jnp.exp2(sc - m)
    pn = jnp.exp2(sn - m)
    l = jnp.sum(pc, axis=1, keepdims=True) + jnp.sum(pn, axis=1, keepdims=True)
    o_lat = ((_dot(pc.astype(BF16), cl) + _dot(pn.astype(BF16), ncl)) * (1.0 / l)).astype(BF16)
    for hp in range(C_HEADS // 2):
        sl = slice(hp * LANES, (hp + 1) * LANES)
        oa = _dot(o_lat[(2 * hp) * st:(2 * hp + 1) * st, :], wuv_ref[hp])
        ob = _dot(o_lat[(2 * hp + 1) * st:(2 * hp + 2) * st, :], wuv_ref[hp])
        o_ref[0, :, sl] = (jnp.where(lo, oa, ob) * sg_ref[0, :, sl].astype(F32)).astype(BF16)


def _stick_unit(q, k, v, negu, run, tri):
    z = _dot_nt(q, k)
    sp = _softplus(z)
    msp = sp if tri is None else jnp.where(tri, sp, 0.0)
    later = _dot(msp.astype(BF16), negu)
    x = z - sp + later
    if run is not None:
        x = x + run
    a = jnp.exp(x)
    if tri is not None:
        a = jnp.where(tri, a, 0.0)
    total = later[:, 0:1] - msp[:, 0:1]
    return total, _dot(a.astype(BF16), v)


def _neg_suffix(n):
    r = lax.broadcasted_iota(jnp.int32, (n, n), 0)
    c = lax.broadcasted_iota(jnp.int32, (n, n), 1)
    return jnp.where(r > c, -1.0, 0.0).astype(BF16)


def _samp_d_kernel(q_ref, ck_ref, cv_ref, nk_ref, nv_ref, sg_ref, o_ref, *, past, st, tk):
    lo = _lane_lo()
    negu_new = _neg_suffix(st)
    r = lax.broadcasted_iota(jnp.int32, (st, st), 0)
    c = lax.broadcasted_iota(jnp.int32, (st, st), 1)
    tri = c < r
    negu = _neg_suffix(tk)
    n_chunks = past // tk
    for hp in range(D_HEADS // 2):
        sl = slice(hp * LANES, (hp + 1) * LANES)
        kn, vn = nk_ref[0, :, sl].astype(BF16), nv_ref[0, :, sl].astype(BF16)
        qa, qb = _head_split(q_ref[0, :, sl])
        run_a, acc_a = _stick_unit(qa, kn, vn, negu_new, None, tri)
        run_b, acc_b = _stick_unit(qb, kn, vn, negu_new, None, tri)

        def cond(carry):
            return jnp.logical_and(carry[0] < n_chunks, carry[1] > -104.0)

        def body(carry, qa=qa, qb=qb, sl=sl):
            n, _, run_a, acc_a, run_b, acc_b = carry
            rows = pl.ds(pl.multiple_of((n_chunks - 1 - n) * tk, tk), tk)
            kc, vc = ck_ref[0, rows, sl].astype(BF16), cv_ref[0, rows, sl].astype(BF16)
            da, ca = _stick_unit(qa, kc, vc, negu, run_a, None)
            db, cb = _stick_unit(qb, kc, vc, negu, run_b, None)
            run_a, run_b = run_a + da, run_b + db
            return n + 1, jnp.maximum(jnp.max(run_a), jnp.max(run_b)), run_a, acc_a + ca, run_b, acc_b + cb

        init = (jnp.int32(0), jnp.maximum(jnp.max(run_a), jnp.max(run_b)), run_a, acc_a, run_b, acc_b)
        _, _, _, acc_a, _, acc_b = lax.while_loop(cond, body, init)
        o = jnp.where(lo, acc_a, acc_b)
        o_ref[0, :, sl] = (o * sg_ref[0, :, sl].astype(F32)).astype(BF16)


def _samp_call(body, name, b, st, ins, specs):
    return pl.pallas_call(
        body,
        grid=(b,),
        in_specs=specs,
        out_specs=pl.BlockSpec((1, st, 512), lambda bi: (bi, 0, 0)),
        out_shape=jax.ShapeDtypeStruct((b, st, 512), BF16),
        compiler_params=_params(("arbitrary",)),
        name=name,
    )(*ins)


def _per_batch(shape, col=0):
    return pl.BlockSpec((1,) + tuple(shape[1:]), lambda bi: (bi,) + (0,) * (len(shape) - 2) + (col,))


def _whole(shape):
    return pl.BlockSpec(tuple(shape), lambda bi: (0,) * len(shape))


def _rope_tables(pos, rot, period, offset):
    half = rot // 2
    inv_freq = ROPE_THETA ** (-jnp.arange(half, dtype=F32) / half)
    ang = pos.astype(F32)[:, None] * inv_freq[None, :]
    cos, sin = jnp.cos(ang), jnp.sin(ang)
    n = pos.shape[0]
    c = jnp.ones((n, period), F32).at[:, offset:offset + half].set(cos).at[:, offset + half:offset + rot].set(cos)
    s1 = jnp.zeros((n, period), F32).at[:, offset:offset + half].set(-sin)
    s2 = jnp.zeros((n, period), F32).at[:, offset + half:offset + rot].set(sin)
    reps = LANES // period
    return tuple(jnp.tile(x, (1, reps)) for x in (c, s1, s2))


def kernel(x_prompt, x_sample, cache_a_k, cache_a_v, cache_b_k, cache_b_v, cache_c_latent, cache_c_krope,
           cache_d_k, cache_d_v, g_pre0, w_in0, lam_q1, lam_k1, lam_q2, lam_k2, g_sub_a, rel_bias_b, w_out0,
           g_post0, g_pre1, w_in1, g_cq, w_uq, g_ckv, w_uk, w_uv, w_out1, g_post1):
    b, t, _ = x_prompt.shape
    sb, st, _ = x_sample.shape
    past = cache_a_k.shape[1]
    band = cache_b_k.shape[1]
    assert band == B_BAND and t % TQ == 0 and past % TQ_D == 0
    ns = sb * st
    row = lambda x: x.reshape(1, -1)

    w0 = w_in0.astype(BF16)
    wo0 = w_out0.astype(BF16)
    wo1 = w_out1.astype(BF16)
    w1 = jnp.concatenate([w_in1[:, 0:1024], w_in1[:, 1056:3616], jnp.zeros((D_MODEL, 64), F32),
                          w_in1[:, 1024:1056], jnp.zeros((D_MODEL, 32), F32)], axis=1).astype(BF16)
    wuq = jnp.pad(w_uq.reshape(C_QL, C_HEADS, C_NOPE + C_ROPE), ((0, 0), (0, 0), (0, 32)))
    wuq = wuq.reshape(C_QL, C_HEADS * LANES).astype(BF16)
    wuk = jnp.pad(w_uk.transpose(1, 0, 2), ((0, 0), (0, 0), (0, LANES - C_NOPE))).astype(BF16)
    wuv = w_uv.reshape(C_KVL, C_HEADS // 2, 2 * C_V).transpose(1, 0, 2).astype(BF16)
    n_u = 2 * min(TQ_D, t // 2)
    negut = -jnp.triu(jnp.ones((n_u, n_u), BF16), 1)
    lams = tuple(row(x) for x in (lam_q1, lam_k1, lam_q2, lam_k2))
    lam_init = 0.8 - 0.6 * math.exp(-0.3 * 0)

    pos_p = jnp.arange(t)
    pos_s = past + jnp.arange(st)
    tabs0_p = _rope_tables(pos_p, A_ROT, A_QK, 0)
    tabs0_s = tuple(jnp.tile(x, (sb, 1)) for x in _rope_tables(pos_s, A_ROT, A_QK, 0))
    tabs1_p = _rope_tables(pos_p, C_ROPE, LANES, C_NOPE)
    tabs1_s = tuple(jnp.tile(x, (sb, 1)) for x in _rope_tables(pos_s, C_ROPE, LANES, C_NOPE))

    bias_t, bias_s = _bias_tiles(rel_bias_b, st)

    aq, ak, av, bq, bk, bv, sg0 = _proj0(x_prompt, row(g_pre0), w0, tabs0_p, 512)
    oa = _attn_a(aq, ak, av, lams, row(g_sub_a), sg0, lam_init)
    ob = _attn_b(bq, bk, bv, bias_t, sg0)
    h1 = _outproj(oa.reshape(b * t, 512), ob.reshape(b * t, 512), wo0, row(g_post0),
                  x_prompt.reshape(b * t, D_MODEL), 512)

    xs = x_sample.reshape(1, ns, D_MODEL)
    s_out = _proj0(xs, row(g_pre0), w0, tabs0_s, ns)
    aq_s, ak_s, av_s, bq_s, bk_s, bv_s, sg0_s = (x.reshape(sb, st, -1) for x in s_out)
    vec = lambda n: _whole((1, n))
    oa_s = _samp_call(
        functools.partial(_samp_a_kernel, lam_init=lam_init), "samp_a", sb, st,
        (aq_s, cache_a_k.reshape(sb, past, 512), cache_a_v.reshape(sb, past, 512), ak_s, av_s,
         *lams, row(g_sub_a), sg0_s),
        [_per_batch((sb, st, 512)), _per_batch((sb, past, 512)), _per_batch((sb, past, 512)),
         _per_batch((sb, st, 512)), _per_batch((sb, st, 512)), vec(A_QK), vec(A_QK), vec(A_QK), vec(A_QK),
         vec(A_V), _per_batch((sb, st, 512))])
    ob_s = _samp_call(
        functools.partial(_samp_b_kernel, band=band, st=st), "samp_b", sb, st,
        (bq_s, cache_b_k.reshape(sb, band, 512), cache_b_v.reshape(sb, band, 512), bk_s, bv_s, bias_s, sg0_s),
        [_per_batch((sb, st, 512)), _per_batch((sb, band, 512)), _per_batch((sb, band, 512)),
         _per_batch((sb, st, 512)), _per_batch((sb, st, 512)),
         pl.BlockSpec((B_HEADS, st, WIN_B), lambda bi: (0, 0, 0)), _per_batch((sb, st, 512), col=1)])
    hs1 = _outproj(oa_s.reshape(ns, 512), ob_s.reshape(ns, 512), wo0, row(g_post0),
                   x_sample.reshape(ns, D_MODEL), ns)

    qc, clat, kr, dq, dk, dv, sg1 = _proj1(h1.reshape(b, t, D_MODEL), row(g_pre1), w1, row(g_cq), wuq,
                                           row(g_ckv), tabs1_p, 512)
    oc = _attn_c(qc, clat, kr, wuk, wuv, sg1)
    od = _attn_d(dq, dk, dv, negut, sg1)
    h2 = _outproj(oc.reshape(b * t, 512), od.reshape(b * t, 512), wo1, row(g_post1), h1, 512)

    s_out = _proj1(hs1.reshape(1, ns, D_MODEL), row(g_pre1), w1, row(g_cq), wuq, row(g_ckv), tabs1_s, ns)
    qc_s, clat_s, kr_s, dq_s, dk_s, dv_s, sg1_s = (x.reshape(sb, st, -1) for x in s_out)
    krp = jnp.pad(cache_c_krope, ((0, 0), (0, 0), (C_NOPE, LANES - C_NOPE - C_ROPE))).astype(BF16)
    wukt = wuk.transpose(0, 2, 1)
    oc_s = _samp_call(
        functools.partial(_samp_c_kernel, st=st), "samp_c", sb, st,
        (qc_s, cache_c_latent, krp, clat_s, kr_s, wukt, wuv, sg1_s),
        [_per_batch((sb, st, 1024)), _per_batch((sb, past, C_KVL)), _per_batch((sb, past, LANES)),
         _per_batch((sb, st, C_KVL)), _per_batch((sb, st, LANES)), _whole(wukt.shape), _whole(wuv.shape),
         _per_batch((sb, st, 512))])
    od_s = _samp_call(
        functools.partial(_samp_d_kernel, past=past, st=st, tk=TQ_D), "samp_d", sb, st,
        (dq_s, cache_d_k.reshape(sb, past, 512), cache_d_v.reshape(sb, past, 512), dk_s, dv_s, sg1_s),
        [_per_batch((sb, st, 512)), _per_batch((sb, past, 512)), _per_batch((sb, past, 512)),
         _per_batch((sb, st, 512)), _per_batch((sb, st, 512)), _per_batch((sb, st, 512), col=1)])
    hs2 = _outproj(oc_s.reshape(ns, 512), od_s.reshape(ns, 512), wo1, row(g_post1), hs1, ns)

    b_rows = min(B_BAND, t)
    return (h2.reshape(b, t, D_MODEL), hs2.reshape(sb, st, D_MODEL),
            ak.reshape(b, t, A_HEADS, 2, A_QK), av.reshape(b, t, A_HEADS, A_V),
            bk[:, t - b_rows:].reshape(b, b_rows, B_HEADS, B_HD), bv[:, t - b_rows:].reshape(b, b_rows, B_HEADS, B_HD),
            clat, kr[:, :, C_NOPE:C_NOPE + C_ROPE],
            dk.reshape(b, t, D_HEADS, D_HD), dv.reshape(b, t, D_HEADS, D_HD),
            ak_s.reshape(sb, st, A_HEADS, 2, A_QK), av_s.reshape(sb, st, A_HEADS, A_V),
            bk_s.reshape(sb, st, B_HEADS, B_HD), bv_s.reshape(sb, st, B_HEADS, B_HD),
            clat_s, kr_s[:, :, C_NOPE:C_NOPE + C_ROPE],
            dk_s.reshape(sb, st, D_HEADS, D_HD), dv_s.reshape(sb, st, D_HEADS, D_HD))
```

```python
import functools
import math

import jax
import jax.numpy as jnp
from jax import lax
from jax.experimental import pallas as pl
from jax.experimental.pallas import tpu as pltpu

F32 = jnp.float32
BF16 = jnp.bfloat16

D_MODEL = 1024
CHUNK = 64
ROPE_THETA = 500000.0
NORM_EPS = 1e-6
A_HEADS, A_QK, A_V, A_ROT = 4, 64, 128, 16
B_HEADS, B_HD, B_LEFT, B_CLIP = 8, 64, 8, 128
C_HEADS, C_NOPE, C_ROPE, C_V, C_QL, C_KVL = 8, 64, 32, 64, 768, 256
D_HEADS, D_HD = 8, 64
B_BAND = B_LEFT * CHUNK
LANES = 128
NEG = -1e30
VMEM_LIMIT = 56 * 1024 * 1024

TQ = 512
TQ_D = 256
TQ_B = 128
WIN_B = B_BAND + TQ_B
LOG2E = 1.4426950408889634
V_ROWS = 144


def _dot(a, b):
    return jnp.dot(a, b, preferred_element_type=F32)


def _dot_nt(a, b):
    return lax.dot_general(a, b, (((1,), (1,)), ((), ())), preferred_element_type=F32)


def _rms(x, g):
    ms = jnp.mean(x * x, axis=-1, keepdims=True)
    return x * lax.rsqrt(ms + NORM_EPS) * g


def _silu(x):
    return x * (1.0 / (1.0 + jnp.exp(-x)))


def _softplus(z):
    return jnp.maximum(z, 0.0) + jnp.log(1.0 + jnp.exp(-jnp.abs(z)))


def _rope(z, c, s1, s2, shift):
    return z * c + pltpu.roll(z, LANES - shift, 1) * s1 + pltpu.roll(z, shift, 1) * s2


def _lane_lo():
    return lax.broadcasted_iota(jnp.int32, (1, LANES), 1) < 64


def _row_lo():
    return lax.broadcasted_iota(jnp.int32, (LANES, 1), 0) < 64


def _params(sem):
    return pltpu.CompilerParams(dimension_semantics=sem, vmem_limit_bytes=VMEM_LIMIT)


def _proj0_kernel(x_ref, g_ref, w_ref, rc_ref, rs1_ref, rs2_ref,
                  aq_ref, ak_ref, av_ref, bq_ref, bk_ref, bv_ref, sg_ref, avb_ref):
    u = _rms(x_ref[0], g_ref[...]).astype(BF16)
    c, s1, s2 = rc_ref[...], rs1_ref[...], rs2_ref[...]
    zq = _dot(u, w_ref[:, 0:512])
    zk = _dot(u, w_ref[:, 512:1024])
    for v in range(4):
        sl = slice(v * LANES, (v + 1) * LANES)
        aq_ref[0, :, sl] = (_rope(zq[:, sl], c, s1, s2, A_ROT // 2) * (A_QK ** -0.5 * LOG2E)).astype(BF16)
        ak_ref[0, :, sl] = _rope(zk[:, sl], c, s1, s2, A_ROT // 2)
    zv = _dot(u, w_ref[:, 1024:1536])
    avb_ref[0] = zv.astype(BF16)
    for h in range(A_HEADS):
        av_ref[0, :, h, :] = zv[:, h * A_V:(h + 1) * A_V]
    bq_ref[0] = (_dot(u, w_ref[:, 1536:2048]) * (B_HD ** -0.5 * LOG2E)).astype(BF16)
    bk_ref[0] = _dot(u, w_ref[:, 2048:2560])
    bv_ref[0] = _dot(u, w_ref[:, 2560:3072])
    sg_ref[0] = _silu(_dot(u, w_ref[:, 3072:4096])).astype(BF16)


def _proj0(x, g, w, tabs, tm):
    b, t, _ = x.shape
    tok = lambda n: pl.BlockSpec((1, tm, n), lambda ti, bi: (bi, ti, 0))
    const = lambda shape: pl.BlockSpec(shape, lambda ti, bi: (0,) * len(shape))
    tab = pl.BlockSpec((tm, LANES), lambda ti, bi: (ti, 0))
    widths = (512, 512, 512, 512, 512, 512, 1024, 512)
    dts = (BF16, F32, F32, BF16, F32, F32, BF16, BF16)
    av4 = lambda i: i == 2
    return pl.pallas_call(
        _proj0_kernel,
        grid=(t // tm, b),
        in_specs=[tok(D_MODEL), const((1, D_MODEL)), const(w.shape), tab, tab, tab],
        out_specs=[pl.BlockSpec((1, tm, A_HEADS, A_V), lambda ti, bi: (bi, ti, 0, 0)) if av4(i) else tok(n)
                   for i, n in enumerate(widths)],
        out_shape=[jax.ShapeDtypeStruct((b, t, A_HEADS, A_V) if av4(i) else (b, t, n), d)
                   for i, (n, d) in enumerate(zip(widths, dts))],
        compiler_params=_params(("arbitrary", "arbitrary")),
        name="proj0",
    )(x, g, w, *tabs)


def _proj1_kernel(x_ref, g_ref, w_ref, gcq_ref, wuq_ref, gckv_ref, rc_ref, rs1_ref, rs2_ref,
                  qc_ref, clat_ref, kr_ref, dq_ref, dk_ref, dv_ref, sg_ref):
    u = _rms(x_ref[0], g_ref[...]).astype(BF16)
    c, s1, s2 = rc_ref[...], rs1_ref[...], rs2_ref[...]
    cq = _rms(_dot(u, w_ref[:, 0:768]), gcq_ref[...]).astype(BF16)
    clat_ref[0] = _rms(_dot(u, w_ref[:, 768:1024]), gckv_ref[...])
    dq_ref[0] = (_dot(u, w_ref[:, 1024:1536]) * (D_HD ** -0.5)).astype(BF16)
    dk_ref[0] = _dot(u, w_ref[:, 1536:2048])
    dv_ref[0] = _dot(u, w_ref[:, 2048:2560])
    sg_ref[0] = _silu(_dot(u, w_ref[:, 2560:3584])).astype(BF16)
    kr_ref[0] = _rope(_dot(u, w_ref[:, 3584:3712]), c, s1, s2, C_ROPE // 2)
    qc = _dot(cq, wuq_ref[...])
    for h in range(C_HEADS):
        sl = slice(h * LANES, (h + 1) * LANES)
        qc_ref[0, :, sl] = (_rope(qc[:, sl], c, s1, s2, C_ROPE // 2)
                            * ((C_NOPE + C_ROPE) ** -0.5 * LOG2E)).astype(BF16)


def _proj1(x, g, w, gcq, wuq, gckv, tabs, tm):
    b, t, _ = x.shape
    tok = lambda n: pl.BlockSpec((1, tm, n), lambda ti, bi: (bi, ti, 0))
    const = lambda shape: pl.BlockSpec(shape, lambda ti, bi: (0,) * len(shape))
    tab = pl.BlockSpec((tm, LANES), lambda ti, bi: (ti, 0))
    widths = (1024, 256, 128, 512, 512, 512, 1024)
    dts = (BF16, F32, F32, BF16, F32, F32, BF16)
    return pl.pallas_call(
        _proj1_kernel,
        grid=(t // tm, b),
        in_specs=[tok(D_MODEL), const((1, D_MODEL)), const(w.shape), const((1, C_QL)),
                  const(wuq.shape), const((1, C_KVL)), tab, tab, tab],
        out_specs=[tok(n) for n in widths],
        out_shape=[jax.ShapeDtypeStruct((b, t, n), d) for n, d in zip(widths, dts)],
        compiler_params=_params(("arbitrary", "arbitrary")),
        name="proj1",
    )(x, g, w, gcq, wuq, gckv, *tabs)


def _outproj_kernel(o1_ref, o2_ref, w_ref, g_ref, h_ref, out_ref):
    y = _dot(o1_ref[...], w_ref[0:512, :]) + _dot(o2_ref[...], w_ref[512:1024, :])
    out_ref[...] = h_ref[...] + _rms(y, g_ref[...])


def _outproj(o1, o2, w, g, h, tm):
    n = h.shape[0]
    row = lambda c: pl.BlockSpec((tm, c), lambda i: (i, 0))
    const = lambda shape: pl.BlockSpec(shape, lambda i: (0,) * len(shape))
    return pl.pallas_call(
        _outproj_kernel,
        grid=(n // tm,),
        in_specs=[row(512), row(512), const(w.shape), const((1, D_MODEL)), row(D_MODEL)],
        out_specs=row(D_MODEL),
        out_shape=jax.ShapeDtypeStruct((n, D_MODEL), F32),
        compiler_params=_params(("arbitrary",)),
        name="outproj",
    )(o1, o2, w, g, h)


def _bias_lookup(rb_ref, h, idx):
    def body(r, acc):
        return jnp.where(idx == r, rb_ref[h, r], acc)

    return lax.fori_loop(0, 2 * B_CLIP + 1, body, jnp.zeros(idx.shape, F32)) * LOG2E


def _bias_kernel(rb_ref, out_t_ref, out_s_ref, *, st):
    h = pl.program_id(0)
    j = lax.broadcasted_iota(jnp.int32, (WIN_B, TQ_B), 0)
    i = lax.broadcasted_iota(jnp.int32, (WIN_B, TQ_B), 1)
    cj = lax.shift_right_logical(j, 6)
    ci = lax.shift_right_logical(i, 6)
    band = jnp.logical_and(cj >= ci, cj <= ci + B_LEFT)
    idx = jnp.clip(B_BAND + i - j, -B_CLIP, B_CLIP) + B_CLIP
    out_t_ref[0] = jnp.where(band, _bias_lookup(rb_ref, h, idx), NEG)
    i = lax.broadcasted_iota(jnp.int32, (st, WIN_B), 0)
    j = lax.broadcasted_iota(jnp.int32, (st, WIN_B), 1)
    out_s_ref[0] = _bias_lookup(rb_ref, h, jnp.clip(B_BAND + i - j, -B_CLIP, B_CLIP) + B_CLIP)


def _bias_tiles(rel_bias, st):
    return pl.pallas_call(
        functools.partial(_bias_kernel, st=st),
        grid=(B_HEADS,),
        in_specs=[pl.BlockSpec(memory_space=pltpu.SMEM)],
        out_specs=[pl.BlockSpec((1, WIN_B, TQ_B), lambda h: (h // 2, 0, h % 2)),
                   pl.BlockSpec((1, st, WIN_B), lambda h: (h, 0, 0))],
        out_shape=[jax.ShapeDtypeStruct((B_HEADS // 2, WIN_B, 2 * TQ_B), F32),
                   jax.ShapeDtypeStruct((B_HEADS, st, WIN_B), F32)],
        compiler_params=_params(("arbitrary",)),
        name="bias_tile",
    )(rel_bias)


def _ones_rows(n):
    r = lax.broadcasted_iota(jnp.int32, (V_ROWS - LANES, n), 0)
    return jnp.where(r == 0, 1.0, 0.0).astype(BF16)


def _causal_attend_all(n_tiles, n_chains, tq, q_tile, k_rows, vt_cols, finish):
    half = tq // 2
    steps = []
    for qi in range(n_tiles):
        steps += [(qi, j * tq, tq, 0, False) for j in range(qi)]
        steps += [(qi, qi * tq, half, 0, True), (qi, qi * tq + half, half, half, True)]
    qcache = {}

    def scores(step):
        qi, k0, nk, q0, _ = step
        for c in range(n_chains):
            if (c, qi) not in qcache:
                qcache[(c, qi)] = q_tile(c, qi)
        return [_dot_nt(k_rows(c, k0, nk), qcache[(c, qi)][q0:tq, :]) for c in range(n_chains)]

    def chunk_mask(step):
        _, k0, nk, q0, _ = step
        j = lax.broadcasted_iota(jnp.int32, (nk, tq - q0), 0) + (k0 % tq)
        i = lax.broadcasted_iota(jnp.int32, (nk, tq - q0), 1) + q0
        return lax.shift_right_logical(j, 6) <= lax.shift_right_logical(i, 6)

    nxt = scores(steps[0])
    m, acc = [None] * n_chains, [None] * n_chains
    for idx, step in enumerate(steps):
        qi, k0, nk, q0, masked = step
        sts = nxt
        if idx + 1 < len(steps):
            nxt = scores(steps[idx + 1])
        if masked:
            msk = chunk_mask(step)
            sts = [jnp.where(msk, st, NEG) for st in sts]
        first = idx == 0 or steps[idx - 1][0] != qi
        pts, alphas = [], []
        for c in range(n_chains):
            top = jnp.max(sts[c], axis=0, keepdims=True)
            if not first:
                old = m[c][:, q0:tq]
                top = jnp.maximum(old, top)
                alphas.append(jnp.exp2(old - top))
                m[c] = top if q0 == 0 else jnp.concatenate([m[c][:, 0:q0], top], axis=1)
            else:
                m[c] = top
            pts.append(jnp.exp2(sts[c] - top).astype(BF16))
        vt = vt_cols(k0, nk)
        for c in range(n_chains):
            pv = _dot(vt, pts[c])
            if first:
                acc[c] = pv
            elif q0 == 0:
                acc[c] = alphas[c] * acc[c] + pv
            else:
                acc[c] = jnp.concatenate([acc[c][:, 0:q0], alphas[c] * acc[c][:, q0:tq] + pv], axis=1)
        if idx + 1 == len(steps) or steps[idx + 1][0] != qi:
            finish(qi, [a[0:LANES, :] * (1.0 / a[LANES:LANES + 1, :]) for a in acc])


def _lam(lq1, lk1, lq2, lk2, lam_init):
    return (jnp.exp(jnp.sum(lq1[...] * lk1[...], axis=1, keepdims=True))
            - jnp.exp(jnp.sum(lq2[...] * lk2[...], axis=1, keepdims=True)) + lam_init)


def _head_split(q):
    lo = _lane_lo()
    zero = jnp.zeros_like(q)
    return [jnp.where(lo, q, zero), jnp.where(lo, zero, q)]


def _attn_a_kernel(q_ref, k_ref, v_ref, lq1, lk1, lq2, lk2, gsub_ref, sg_ref, o_ref, kb_ref, vt_ref,
                   *, tq, t, lam_init):
    kb_ref[...] = k_ref[0].astype(BF16)
    vt_ref[0:LANES, :] = v_ref[0].T.astype(BF16)
    vt_ref[LANES:V_ROWS, :] = _ones_rows(t)
    lam = _lam(lq1, lk1, lq2, lk2, lam_init)
    tile = lambda i: slice(i * tq, (i + 1) * tq)
    split = {}

    def q_tile(c, qi):
        if qi not in split:
            split[qi] = _head_split(q_ref[0, tile(qi), :])
        return split[qi][c]

    def finish(qi, outs):
        o = outs[0].T - lam * outs[1].T
        o = _rms(o, gsub_ref[...]) * (1.0 - lam_init)
        o_ref[0, tile(qi), :] = (o * sg_ref[0, tile(qi), :].astype(F32)).astype(BF16)

    _causal_attend_all(t // tq, 2, tq, q_tile, lambda c, k0, n: kb_ref[k0:k0 + n, :],
                       lambda k0, n: vt_ref[:, k0:k0 + n], finish)


def _attn_a(q, k, v, lams, gsub, sg, lam_init):
    b, t, _ = q.shape
    tq = min(TQ, t)
    blk = pl.BlockSpec((1, t, LANES), lambda bi, h: (bi, 0, h))
    vec = lambda n: pl.BlockSpec((1, n), lambda bi, h: (0, 0))
    return pl.pallas_call(
        functools.partial(_attn_a_kernel, tq=tq, t=t, lam_init=lam_init),
        grid=(b, A_HEADS),
        in_specs=[blk, blk, blk, vec(A_QK), vec(A_QK), vec(A_QK), vec(A_QK), vec(A_V), blk],
        out_specs=blk,
        out_shape=jax.ShapeDtypeStruct((b, t, A_HEADS * A_V), BF16),
        scratch_shapes=[pltpu.VMEM((t, LANES), BF16), pltpu.VMEM((V_ROWS, t), BF16)],
        compiler_params=_params(("arbitrary", "arbitrary")),
        name="attn_a",
    )(q, k, v, *lams, gsub, sg)


def _attn_b_kernel(q_ref, k_ref, v_ref, bias_ref, sg_ref, o_ref, kb_ref, vt_ref, *, t):
    kb_ref[0:B_BAND, :] = jnp.zeros((B_BAND, LANES), BF16)
    kb_ref[B_BAND:B_BAND + t, :] = k_ref[0].astype(BF16)
    vt_ref[:, 0:B_BAND] = jnp.zeros((V_ROWS, B_BAND), BF16)
    vt_ref[0:LANES, B_BAND:B_BAND + t] = v_ref[0].T.astype(BF16)
    vt_ref[LANES:V_ROWS, B_BAND:B_BAND + t] = _ones_rows(t)
    key = lax.broadcasted_iota(jnp.int32, (WIN_B, 1), 0)
    row_lo = _row_lo()

    def tiles(q0s, first_valids):
        def scores(n):
            qa, qb = _head_split(q_ref[0, pl.ds(q0s[n], TQ_B), :])
            st = _dot_nt(kb_ref[pl.ds(q0s[n], WIN_B), :], jnp.concatenate([qa, qb], axis=0)) + bias_ref[0]
            if first_valids[n] > 0:
                st = jnp.where(key < first_valids[n], NEG, st)
            return st

        nxt = scores(0)
        for n in range(len(q0s)):
            st = nxt
            if n + 1 < len(q0s):
                nxt = scores(n + 1)
            pt = jnp.exp2(st - jnp.max(st, axis=0, keepdims=True)).astype(BF16)
            acc = _dot(vt_ref[:, pl.ds(q0s[n], WIN_B)], pt)
            inv = 1.0 / acc[LANES:LANES + 1, :]
            ot = jnp.where(row_lo, acc[0:LANES, 0:TQ_B] * inv[:, 0:TQ_B],
                           acc[0:LANES, TQ_B:2 * TQ_B] * inv[:, TQ_B:2 * TQ_B])
            rows = pl.ds(q0s[n], TQ_B)
            o_ref[0, rows, :] = (ot.T * sg_ref[0, rows, :].astype(F32)).astype(BF16)

    n_edge = min(B_BAND, t) // TQ_B
    tiles([qi * TQ_B for qi in range(n_edge)], [B_BAND - qi * TQ_B for qi in range(n_edge)])
    per_iter = 4

    def body(i, carry):
        q0 = pl.multiple_of(n_edge * TQ_B + i * per_iter * TQ_B, per_iter * TQ_B)
        tiles([q0 + n * TQ_B for n in range(per_iter)], [0] * per_iter)
        return carry

    lax.fori_loop(0, (t - n_edge * TQ_B) // (per_iter * TQ_B), body, 0)


def _attn_b(q, k, v, bias_t, sg):
    b, t, _ = q.shape
    assert t % (4 * TQ_B) == 0
    blk = pl.BlockSpec((1, t, LANES), lambda bi, h: (bi, 0, h))
    sgblk = pl.BlockSpec((1, t, LANES), lambda bi, h: (bi, 0, h + A_HEADS))
    bblk = pl.BlockSpec((1, WIN_B, 2 * TQ_B), lambda bi, h: (h, 0, 0))
    return pl.pallas_call(
        functools.partial(_attn_b_kernel, t=t),
        grid=(b, B_HEADS // 2),
        in_specs=[blk, blk, blk, bblk, sgblk],
        out_specs=blk,
        out_shape=jax.ShapeDtypeStruct((b, t, B_HEADS * B_HD), BF16),
        scratch_shapes=[pltpu.VMEM((B_BAND + t, LANES), BF16), pltpu.VMEM((V_ROWS, B_BAND + t), BF16)],
        compiler_params=_params(("arbitrary", "arbitrary")),
        name="attn_b",
    )(q, k, v, bias_t, sg)


def _attn_c_kernel(q_ref, clat_ref, kr_ref, wuk_ref, wuv_ref, sg_ref, o_ref, kc_ref, vt_ref, *, tq, t):
    cl = clat_ref[0].astype(BF16)
    kr = kr_ref[0]
    for hh in range(2):
        kc_ref[hh] = (_dot(cl, wuk_ref[hh]) + kr).astype(BF16)
    vt_ref[0:LANES, :] = _dot(cl, wuv_ref[0]).T.astype(BF16)
    vt_ref[LANES:V_ROWS, :] = _ones_rows(t)
    tile = lambda i: slice(i * tq, (i + 1) * tq)
    row_lo = _row_lo()

    def finish(qi, outs):
        o = jnp.where(row_lo, outs[0], outs[1]).T
        o_ref[0, tile(qi), :] = (o * sg_ref[0, tile(qi), :].astype(F32)).astype(BF16)

    _causal_attend_all(t // tq, 2, tq, lambda c, qi: q_ref[0, tile(qi), c * LANES:(c + 1) * LANES],
                       lambda c, k0, n: kc_ref[c, k0:k0 + n, :], lambda k0, n: vt_ref[:, k0:k0 + n], finish)


def _attn_c(qc, clat, kr, wuk, wuv, sg):
    b, t, _ = qc.shape
    tq = min(TQ, t)
    return pl.pallas_call(
        functools.partial(_attn_c_kernel, tq=tq, t=t),
        grid=(b, C_HEADS // 2),
        in_specs=[pl.BlockSpec((1, t, 2 * LANES), lambda bi, h: (bi, 0, h)),
                  pl.BlockSpec((1, t, C_KVL), lambda bi, h: (bi, 0, 0)),
                  pl.BlockSpec((1, t, LANES), lambda bi, h: (bi, 0, 0)),
                  pl.BlockSpec((2, C_KVL, LANES), lambda bi, h: (h, 0, 0)),
                  pl.BlockSpec((1, C_KVL, LANES), lambda bi, h: (h, 0, 0)),
                  pl.BlockSpec((1, t, LANES), lambda bi, h: (bi, 0, h))],
        out_specs=pl.BlockSpec((1, t, LANES), lambda bi, h: (bi, 0, h)),
        out_shape=jax.ShapeDtypeStruct((b, t, C_HEADS * C_V), BF16),
        scratch_shapes=[pltpu.VMEM((2, t, LANES), BF16), pltpu.VMEM((V_ROWS, t), BF16)],
        compiler_params=_params(("arbitrary", "arbitrary")),
        name="attn_c",
    )(qc, clat, kr, wuk, wuv, sg)


def _stick_units_t(qs, ks, vts, negut, runs, tri_t):
    n = len(qs)
    zts = [_dot_nt(ks[c], qs[c]) for c in range(n)]
    sps = [_softplus(zt) for zt in zts]
    msps = sps if tri_t is None else [jnp.where(tri_t, sp, 0.0) for sp in sps]
    laters = [_dot(negut, msp.astype(BF16)) for msp in msps]
    outs = []
    for c in range(n):
        x = zts[c] - sps[c] + laters[c]
        if runs is not None:
            x = x + runs[c]
        a = jnp.exp(x)
        if tri_t is not None:
            a = jnp.where(tri_t, a, 0.0)
        outs.append((laters[c][0:1, :] - msps[c][0:1, :], a.astype(BF16)))
    return [(outs[c][0], _dot(vts[c], outs[c][1])) for c in range(n)]


def _attn_d_kernel(q_ref, k_ref, v_ref, negut_ref, sg_ref, o_ref, kb_ref, vt_ref, *, tk, t):
    qt = pl.program_id(2)

    @pl.when(qt == 0)
    def _():
        kb_ref[0:tk, :] = jnp.zeros((tk, LANES), BF16)
        vt_ref[:, 0:tk] = jnp.zeros((LANES, tk), BF16)
        kb_ref[tk:tk + t, :] = k_ref[0].astype(BF16)
        vt_ref[:, tk:tk + t] = v_ref[0].T.astype(BF16)

    qs = _head_split(q_ref[0, 0:tk, :]) + _head_split(q_ref[0, tk:2 * tk, :])
    half_of = (0, 0, 1, 1)
    j = lax.broadcasted_iota(jnp.int32, (2 * tk, tk), 0)
    i = lax.broadcasted_iota(jnp.int32, (2 * tk, tk), 1)
    tri_t = j < i + tk
    wins = [pl.ds(pl.multiple_of((2 * qt + h) * tk, tk), 2 * tk) for h in half_of]
    first = _stick_units_t(qs, [kb_ref[w, :] for w in wins], [vt_ref[:, w] for w in wins], negut_ref[...],
                           None, tri_t)
    runs = [f[0] for f in first]
    accs = [f[1] for f in first]

    def top(rs):
        return functools.reduce(jnp.maximum, [jnp.max(r) for r in rs])

    def cond(carry):
        return jnp.logical_and(carry[0] < 2 * qt, carry[1] > -104.0)

    def body(carry):
        n, _, runs, accs = carry
        tiles = [pl.ds(pl.multiple_of((2 * qt + h - 1 - n) * tk, tk), tk) for h in half_of]
        res = _stick_units_t(qs, [kb_ref[w, :] for w in tiles], [vt_ref[:, w] for w in tiles],
                             negut_ref[0:tk, 0:tk], runs, None)
        runs = [r + d for r, (d, _) in zip(runs, res)]
        accs = [a + c for a, (_, c) in zip(accs, res)]
        return n + 1, top(runs), runs, accs

    _, _, _, accs = lax.while_loop(cond, body, (jnp.int32(0), top(runs), runs, accs))
    row_lo = _row_lo()
    for h in range(2):
        ot = jnp.where(row_lo, accs[2 * h], accs[2 * h + 1])
        rows = slice(h * tk, (h + 1) * tk)
        o_ref[0, rows, :] = (ot.T * sg_ref[0, rows, :].astype(F32)).astype(BF16)


def _attn_d(q, k, v, negut, sg):
    b, t, _ = q.shape
    tk = min(TQ_D, t // 2)
    qblk = pl.BlockSpec((1, 2 * tk, LANES), lambda bi, h, qi: (bi, qi, h))
    sgblk = pl.BlockSpec((1, 2 * tk, LANES), lambda bi, h, qi: (bi, qi, h + C_HEADS // 2))
    kvblk = pl.BlockSpec((1, t, LANES), lambda bi, h, qi: (bi, 0, h))
    ublk = pl.BlockSpec((2 * tk, 2 * tk), lambda bi, h, qi: (0, 0))
    return pl.pallas_call(
        functools.partial(_attn_d_kernel, tk=tk, t=t),
        grid=(b, D_HEADS // 2, t // (2 * tk)),
        in_specs=[qblk, kvblk, kvblk, ublk, sgblk],
        out_specs=qblk,
        out_shape=jax.ShapeDtypeStruct((b, t, D_HEADS * D_HD), BF16),
        scratch_shapes=[pltpu.VMEM((tk + t, LANES), BF16), pltpu.VMEM((LANES, tk + t), BF16)],
        compiler_params=_params(("arbitrary", "arbitrary", "arbitrary")),
        name="attn_d",
    )(q, k, v, negut, sg)


def _two_part_softmax(q, kct, kn, vc, vn, bc=None, bn=None, vc_t=False):
    sc = _dot(q, kct)
    sn = _dot_nt(q, kn)
    if bc is not None:
        sc = sc + bc
        sn = sn + bn
    m = jnp.maximum(jnp.max(sc, axis=1, keepdims=True), jnp.max(sn, axis=1, keepdims=True))
    pc = jnp.exp2(sc - m)
    pn = jnp.exp2(sn - m)
    l = jnp.sum(pc, axis=1, keepdims=True) + jnp.sum(pn, axis=1, keepdims=True)
    pvc = _dot_nt(pc.astype(BF16), vc) if vc_t else _dot(pc.astype(BF16), vc)
    return (pvc + _dot(pn.astype(BF16), vn)) * (1.0 / l)


def _samp_a_kernel(q_ref, ck_ref, cv_ref, nk_ref, nv_ref, lq1, lk1, lq2, lk2, gsub_ref, sg_ref, o_ref,
                   *, lam_init):
    lam = _lam(lq1, lk1, lq2, lk2, lam_init)
    for h in range(A_HEADS):
        sl = slice(h * LANES, (h + 1) * LANES)
        kct, vc = ck_ref[0, h].astype(BF16), cv_ref[0, :, h, :].astype(BF16)
        kn, vn = nk_ref[0, :, sl].astype(BF16), nv_ref[0, :, sl].astype(BF16)
        q0, q1 = _head_split(q_ref[0, :, sl])
        o0 = _two_part_softmax(q0, kct, kn, vc, vn)
        o1 = _two_part_softmax(q1, kct, kn, vc, vn)
        o = _rms(o0 - lam * o1, gsub_ref[...]) * (1.0 - lam_init)
        o_ref[0, :, sl] = (o * sg_ref[0, :, sl].astype(F32)).astype(BF16)


def _samp_b_kernel(q_ref, ck_ref, cv_ref, nk_ref, nv_ref, bias_ref, sg_ref, o_ref, *, band, st):
    lo = _lane_lo()
    for hp in range(B_HEADS // 2):
        sl = slice(hp * LANES, (hp + 1) * LANES)
        kct, vct = ck_ref[0, hp].astype(BF16), cv_ref[0, hp].astype(BF16)
        kn, vn = nk_ref[0, :, sl].astype(BF16), nv_ref[0, :, sl].astype(BF16)
        outs = []
        for hh, qm in enumerate(_head_split(q_ref[0, :, sl])):
            bias = bias_ref[2 * hp + hh]
            outs.append(_two_part_softmax(qm, kct, kn, vct, vn, bias[:, 0:band], bias[:, band:band + st],
                                          vc_t=True))
        o = jnp.where(lo, outs[0], outs[1])
        o_ref[0, :, sl] = (o * sg_ref[0, :, sl].astype(F32)).astype(BF16)


def _samp_c_kernel(q_ref, clat_ref, krp_ref, nclat_ref, nkr_ref, wukt_ref, wuv_ref, sg_ref, o_ref, *, st):
    lo = _lane_lo()
    cl = clat_ref[0].astype(BF16)
    ncl = nclat_ref[0].astype(BF16)
    past = krp_ref.shape[2]
    krct = jnp.concatenate([jnp.zeros((C_NOPE, past), BF16), krp_ref[0].astype(BF16),
                            jnp.zeros((LANES - C_NOPE - C_ROPE, past), BF16)], axis=0)
    krn = nkr_ref[0].astype(BF16)
    qh = [q_ref[0, :, h * LANES:(h + 1) * LANES] for h in range(C_HEADS)]
    q_all = jnp.concatenate(qh, axis=0)
    q_lat = jnp.concatenate([_dot(qh[h], wukt_ref[h]) for h in range(C_HEADS)], axis=0).astype(BF16)
    sc = _dot_nt(q_lat, cl) + _dot(q_all, krct)
    sn = _dot_nt(q_lat, ncl) + _dot_nt(q_all, krn)
    m = jnp.maximum(jnp.max(sc, axis=1, keepdims=True), jnp.max(sn, axis=1, keepdims=True))
    pc = jnp.exp2(sc - m)
    pn = jnp.exp2(sn - m)
    l = jnp.sum(pc, axis=1, keepdims=True) + jnp.sum(pn, axis=1, keepdims=True)
    o_lat = ((_dot(pc.astype(BF16), cl) + _dot(pn.astype(BF16), ncl)) * (1.0 / l)).astype(BF16)
    for hp in range(C_HEADS // 2):
        sl = slice(hp * LANES, (hp + 1) * LANES)
        oa = _dot(o_lat[(2 * hp) * st:(2 * hp + 1) * st, :], wuv_ref[hp])
        ob = _dot(o_lat[(2 * hp + 1) * st:(2 * hp + 2) * st, :], wuv_ref[hp])
        o_ref[0, :, sl] = (jnp.where(lo, oa, ob) * sg_ref[0, :, sl].astype(F32)).astype(BF16)


def _stick_unit(q, k, v, negu, run, tri, kv_t=False):
    z = _dot(q, k) if kv_t else _dot_nt(q, k)
    sp = _softplus(z)
    msp = sp if tri is None else jnp.where(tri, sp, 0.0)
    later = _dot(msp.astype(BF16), negu)
    x = z - sp + later
    if run is not None:
        x = x + run
    a = jnp.exp(x)
    if tri is not None:
        a = jnp.where(tri, a, 0.0)
    total = later[:, 0:1] - msp[:, 0:1]
    return total, (_dot_nt(a.astype(BF16), v) if kv_t else _dot(a.astype(BF16), v))


def _neg_suffix(n):
    r = lax.broadcasted_iota(jnp.int32, (n, n), 0)
    c = lax.broadcasted_iota(jnp.int32, (n, n), 1)
    return jnp.where(r > c, -1.0, 0.0).astype(BF16)


def _samp_d_kernel(q_ref, ck_ref, cv_ref, nk_ref, nv_ref, sg_ref, o_ref, *, past, st, tk):
    lo = _lane_lo()
    negu_new = _neg_suffix(st)
    r = lax.broadcasted_iota(jnp.int32, (st, st), 0)
    c = lax.broadcasted_iota(jnp.int32, (st, st), 1)
    tri = c < r
    negu = _neg_suffix(tk)
    n_chunks = past // tk
    for hp in range(D_HEADS // 2):
        sl = slice(hp * LANES, (hp + 1) * LANES)
        kn, vn = nk_ref[0, :, sl].astype(BF16), nv_ref[0, :, sl].astype(BF16)
        qa, qb = _head_split(q_ref[0, :, sl])
        run_a, acc_a = _stick_unit(qa, kn, vn, negu_new, None, tri)
        run_b, acc_b = _stick_unit(qb, kn, vn, negu_new, None, tri)

        def cond(carry):
            return jnp.logical_and(carry[0] < n_chunks, carry[1] > -104.0)

        def body(carry, qa=qa, qb=qb, hp=hp):
            n, _, run_a, acc_a, run_b, acc_b = carry
            keys = pl.ds(pl.multiple_of((n_chunks - 1 - n) * tk, tk), tk)
            kct, vct = ck_ref[0, hp, :, keys].astype(BF16), cv_ref[0, hp, :, keys].astype(BF16)
            da, ca = _stick_unit(qa, kct, vct, negu, run_a, None, kv_t=True)
            db, cb = _stick_unit(qb, kct, vct, negu, run_b, None, kv_t=True)
            run_a, run_b = run_a + da, run_b + db
            return n + 1, jnp.maximum(jnp.max(run_a), jnp.max(run_b)), run_a, acc_a + ca, run_b, acc_b + cb

        init = (jnp.int32(0), jnp.maximum(jnp.max(run_a), jnp.max(run_b)), run_a, acc_a, run_b, acc_b)
        _, _, _, acc_a, _, acc_b = lax.while_loop(cond, body, init)
        o = jnp.where(lo, acc_a, acc_b)
        o_ref[0, :, sl] = (o * sg_ref[0, :, sl].astype(F32)).astype(BF16)


def _samp_call(body, name, b, st, ins, specs):
    return pl.pallas_call(
        body,
        grid=(b,),
        in_specs=specs,
        out_specs=pl.BlockSpec((1, st, 512), lambda bi: (bi, 0, 0)),
        out_shape=jax.ShapeDtypeStruct((b, st, 512), BF16),
        compiler_params=_params(("arbitrary",)),
        name=name,
    )(*ins)


def _per_batch(shape, col=0):
    return pl.BlockSpec((1,) + tuple(shape[1:]), lambda bi: (bi,) + (0,) * (len(shape) - 2) + (col,))


def _whole(shape):
    return pl.BlockSpec(tuple(shape), lambda bi: (0,) * len(shape))


def _rope_tables(pos, rot, period, offset):
    half = rot // 2
    inv_freq = ROPE_THETA ** (-jnp.arange(half, dtype=F32) / half)
    ang = pos.astype(F32)[:, None] * inv_freq[None, :]
    cos, sin = jnp.cos(ang), jnp.sin(ang)
    n = pos.shape[0]
    c = jnp.ones((n, period), F32).at[:, offset:offset + half].set(cos).at[:, offset + half:offset + rot].set(cos)
    s1 = jnp.zeros((n, period), F32).at[:, offset:offset + half].set(-sin)
    s2 = jnp.zeros((n, period), F32).at[:, offset + half:offset + rot].set(sin)
    reps = LANES // period
    return tuple(jnp.tile(x, (1, reps)) for x in (c, s1, s2))


def kernel(x_prompt, x_sample, cache_a_k, cache_a_v, cache_b_k, cache_b_v, cache_c_latent, cache_c_krope,
           cache_d_k, cache_d_v, g_pre0, w_in0, lam_q1, lam_k1, lam_q2, lam_k2, g_sub_a, rel_bias_b, w_out0,
           g_post0, g_pre1, w_in1, g_cq, w_uq, g_ckv, w_uk, w_uv, w_out1, g_post1):
    b, t, _ = x_prompt.shape
    sb, st, _ = x_sample.shape
    past = cache_a_k.shape[1]
    band = cache_b_k.shape[1]
    assert band == B_BAND and t % TQ == 0 and past % TQ_D == 0
    ns = sb * st
    row = lambda x: x.reshape(1, -1)

    w0 = w_in0.astype(BF16)
    wo0 = w_out0.astype(BF16)
    wo1 = w_out1.astype(BF16)
    w1 = jnp.concatenate([w_in1[:, 0:1024], w_in1[:, 1056:3616], jnp.zeros((D_MODEL, 64), F32),
                          w_in1[:, 1024:1056], jnp.zeros((D_MODEL, 32), F32)], axis=1).astype(BF16)
    wuq = jnp.pad(w_uq.reshape(C_QL, C_HEADS, C_NOPE + C_ROPE), ((0, 0), (0, 0), (0, 32)))
    wuq = wuq.reshape(C_QL, C_HEADS * LANES).astype(BF16)
    wuk = jnp.pad(w_uk.transpose(1, 0, 2), ((0, 0), (0, 0), (0, LANES - C_NOPE))).astype(BF16)
    wuv = w_uv.reshape(C_KVL, C_HEADS // 2, 2 * C_V).transpose(1, 0, 2).astype(BF16)
    n_u = 2 * min(TQ_D, t // 2)
    negut = -jnp.triu(jnp.ones((n_u, n_u), BF16), 1)
    lams = tuple(row(x) for x in (lam_q1, lam_k1, lam_q2, lam_k2))
    lam_init = 0.8 - 0.6 * math.exp(-0.3 * 0)

    pos_p = jnp.arange(t)
    pos_s = past + jnp.arange(st)
    tabs0_p = _rope_tables(pos_p, A_ROT, A_QK, 0)
    tabs0_s = tuple(jnp.tile(x, (sb, 1)) for x in _rope_tables(pos_s, A_ROT, A_QK, 0))
    tabs1_p = _rope_tables(pos_p, C_ROPE, LANES, C_NOPE)
    tabs1_s = tuple(jnp.tile(x, (sb, 1)) for x in _rope_tables(pos_s, C_ROPE, LANES, C_NOPE))

    bias_t, bias_s = _bias_tiles(rel_bias_b, st)

    aq, ak, av, bq, bk, bv, sg0, avb = _proj0(x_prompt, row(g_pre0), w0, tabs0_p, 512)
    oa = _attn_a(aq, ak, avb, lams, row(g_sub_a), sg0, lam_init)
    ob = _attn_b(bq, bk, bv, bias_t, sg0)
    h1 = _outproj(oa.reshape(b * t, 512), ob.reshape(b * t, 512), wo0, row(g_post0),
                  x_prompt.reshape(b * t, D_MODEL), 1024)

    def keys_minor(x):
        n = x.shape[1]
        return jnp.moveaxis(x, 1, -1).reshape(sb, -1, LANES, n)

    xs = x_sample.reshape(1, ns, D_MODEL)
    s_out = _proj0(xs, row(g_pre0), w0, tabs0_s, ns)
    aq_s, ak_s, av_s, bq_s, bk_s, bv_s, sg0_s, _ = (x.reshape(sb, st, -1) for x in s_out)
    vec = lambda n: _whole((1, n))
    oa_s = _samp_call(
        functools.partial(_samp_a_kernel, lam_init=lam_init), "samp_a", sb, st,
        (aq_s, keys_minor(cache_a_k), cache_a_v, ak_s, av_s,
         *lams, row(g_sub_a), sg0_s),
        [_per_batch((sb, st, 512)), _per_batch((sb, A_HEADS, LANES, past)), _per_batch((sb, past, A_HEADS, A_V)),
         _per_batch((sb, st, 512)), _per_batch((sb, st, 512)), vec(A_QK), vec(A_QK), vec(A_QK), vec(A_QK),
         vec(A_V), _per_batch((sb, st, 512))])
    ob_s = _samp_call(
        functools.partial(_samp_b_kernel, band=band, st=st), "samp_b", sb, st,
        (bq_s, keys_minor(cache_b_k), keys_minor(cache_b_v), bk_s, bv_s, bias_s, sg0_s),
        [_per_batch((sb, st, 512)), _per_batch((sb, B_HEADS // 2, LANES, band)),
         _per_batch((sb, B_HEADS // 2, LANES, band)),
         _per_batch((sb, st, 512)), _per_batch((sb, st, 512)),
         pl.BlockSpec((B_HEADS, st, WIN_B), lambda bi: (0, 0, 0)), _per_batch((sb, st, 512), col=1)])
    hs1 = _outproj(oa_s.reshape(ns, 512), ob_s.reshape(ns, 512), wo0, row(g_post0),
                   x_sample.reshape(ns, D_MODEL), ns)

    qc, clat, kr, dq, dk, dv, sg1 = _proj1(h1.reshape(b, t, D_MODEL), row(g_pre1), w1, row(g_cq), wuq,
                                           row(g_ckv), tabs1_p, 512)
    oc = _attn_c(qc, clat, kr, wuk, wuv, sg1)
    od = _attn_d(dq, dk, dv, negut, sg1)
    h2 = _outproj(oc.reshape(b * t, 512), od.reshape(b * t, 512), wo1, row(g_post1), h1, 1024)

    s_out = _proj1(hs1.reshape(1, ns, D_MODEL), row(g_pre1), w1, row(g_cq), wuq, row(g_ckv), tabs1_s, ns)
    qc_s, clat_s, kr_s, dq_s, dk_s, dv_s, sg1_s = (x.reshape(sb, st, -1) for x in s_out)
    krt = jnp.moveaxis(cache_c_krope, 1, -1)
    wukt = wuk.transpose(0, 2, 1)
    oc_s = _samp_call(
        functools.partial(_samp_c_kernel, st=st), "samp_c", sb, st,
        (qc_s, cache_c_latent, krt, clat_s, kr_s, wukt, wuv, sg1_s),
        [_per_batch((sb, st, 1024)), _per_batch((sb, past, C_KVL)), _per_batch((sb, C_ROPE, past)),
         _per_batch((sb, st, C_KVL)), _per_batch((sb, st, LANES)), _whole(wukt.shape), _whole(wuv.shape),
         _per_batch((sb, st, 512))])
    od_s = _samp_call(
        functools.partial(_samp_d_kernel, past=past, st=st, tk=TQ_D), "samp_d", sb, st,
        (dq_s, keys_minor(cache_d_k), keys_minor(cache_d_v), dk_s, dv_s, sg1_s),
        [_per_batch((sb, st, 512)), _per_batch((sb, D_HEADS // 2, LANES, past)),
         _per_batch((sb, D_HEADS // 2, LANES, past)),
         _per_batch((sb, st, 512)), _per_batch((sb, st, 512)), _per_batch((sb, st, 512), col=1)])
    hs2 = _outproj(oc_s.reshape(ns, 512), od_s.reshape(ns, 512), wo1, row(g_post1), hs1, ns)

    b_rows = min(B_BAND, t)
    return (h2.reshape(b, t, D_MODEL), hs2.reshape(sb, st, D_MODEL),
            ak.reshape(b, t, A_HEADS, 2, A_QK), av,
            bk[:, t - b_rows:].reshape(b, b_rows, B_HEADS, B_HD), bv[:, t - b_rows:].reshape(b, b_rows, B_HEADS, B_HD),
            clat, kr[:, :, C_NOPE:C_NOPE + C_ROPE],
            dk.reshape(b, t, D_HEADS, D_HD), dv.reshape(b, t, D_HEADS, D_HD),
            ak_s.reshape(sb, st, A_HEADS, 2, A_QK), av_s.reshape(sb, st, A_HEADS, A_V),
            bk_s.reshape(sb, st, B_HEADS, B_HD), bv_s.reshape(sb, st, B_HEADS, B_HD),
            clat_s, kr_s[:, :, C_NOPE:C_NOPE + C_ROPE],
            dk_s.reshape(sb, st, D_HEADS, D_HD), dv_s.reshape(sb, st, D_HEADS, D_HD))
```

```python
import functools
import math

import jax
import jax.numpy as jnp
from jax import lax
from jax.experimental import pallas as pl
from jax.experimental.pallas import tpu as pltpu

F32 = jnp.float32
BF16 = jnp.bfloat16

D_MODEL = 1024
CHUNK = 64
ROPE_THETA = 500000.0
NORM_EPS = 1e-6
A_HEADS, A_QK, A_V, A_ROT = 4, 64, 128, 16
B_HEADS, B_HD, B_LEFT, B_CLIP = 8, 64, 8, 128
C_HEADS, C_NOPE, C_ROPE, C_V, C_QL, C_KVL = 8, 64, 32, 64, 768, 256
D_HEADS, D_HD = 8, 64
B_BAND = B_LEFT * CHUNK
LANES = 128
NEG = -1e30
VMEM_LIMIT = 56 * 1024 * 1024

TQ = 512
TQ_D = 256
NQ_D = 4
TQ_B = 128
WIN_B = B_BAND + TQ_B
LOG2E = 1.4426950408889634
V_ROWS = 144


def _dot(a, b):
    return jnp.dot(a, b, preferred_element_type=F32)


def _dot_nt(a, b):
    return lax.dot_general(a, b, (((1,), (1,)), ((), ())), preferred_element_type=F32)


def _rms(x, g):
    ms = jnp.mean(x * x, axis=-1, keepdims=True)
    return x * lax.rsqrt(ms + NORM_EPS) * g


def _silu(x):
    return x * (1.0 / (1.0 + jnp.exp(-x)))


def _softplus(z):
    return jnp.maximum(z, 0.0) + jnp.log(1.0 + jnp.exp(-jnp.abs(z)))


def _rope(z, c, s1, s2, shift):
    return z * c + pltpu.roll(z, LANES - shift, 1) * s1 + pltpu.roll(z, shift, 1) * s2


def _lane_lo():
    return lax.broadcasted_iota(jnp.int32, (1, LANES), 1) < 64


def _row_lo():
    return lax.broadcasted_iota(jnp.int32, (LANES, 1), 0) < 64


def _params(sem):
    return pltpu.CompilerParams(dimension_semantics=sem, vmem_limit_bytes=VMEM_LIMIT)


def _proj0_kernel(x_ref, g_ref, w_ref, rc_ref, rs1_ref, rs2_ref,
                  aq_ref, ak_ref, av_ref, bq_ref, bk_ref, bv_ref, sg_ref, avb_ref):
    u = _rms(x_ref[0], g_ref[...]).astype(BF16)
    c, s1, s2 = rc_ref[...], rs1_ref[...], rs2_ref[...]
    zq = _dot(u, w_ref[:, 0:512])
    zk = _dot(u, w_ref[:, 512:1024])
    for v in range(4):
        sl = slice(v * LANES, (v + 1) * LANES)
        aq_ref[0, :, sl] = (_rope(zq[:, sl], c, s1, s2, A_ROT // 2) * (A_QK ** -0.5 * LOG2E)).astype(BF16)
        ak_ref[0, :, sl] = _rope(zk[:, sl], c, s1, s2, A_ROT // 2)
    zv = _dot(u, w_ref[:, 1024:1536])
    avb_ref[0] = zv.astype(BF16)
    for h in range(A_HEADS):
        av_ref[0, :, h, :] = zv[:, h * A_V:(h + 1) * A_V]
    bq_ref[0] = (_dot(u, w_ref[:, 1536:2048]) * (B_HD ** -0.5 * LOG2E)).astype(BF16)
    bk_ref[0] = _dot(u, w_ref[:, 2048:2560])
    bv_ref[0] = _dot(u, w_ref[:, 2560:3072])
    sg_ref[0] = _silu(_dot(u, w_ref[:, 3072:4096])).astype(BF16)


def _proj0(x, g, w, tabs, tm):
    b, t, _ = x.shape
    tok = lambda n: pl.BlockSpec((1, tm, n), lambda ti, bi: (bi, ti, 0))
    const = lambda shape: pl.BlockSpec(shape, lambda ti, bi: (0,) * len(shape))
    tab = pl.BlockSpec((tm, LANES), lambda ti, bi: (ti, 0))
    widths = (512, 512, 512, 512, 512, 512, 1024, 512)
    dts = (BF16, F32, F32, BF16, F32, F32, BF16, BF16)
    av4 = lambda i: i == 2
    return pl.pallas_call(
        _proj0_kernel,
        grid=(t // tm, b),
        in_specs=[tok(D_MODEL), const((1, D_MODEL)), const(w.shape), tab, tab, tab],
        out_specs=[pl.BlockSpec((1, tm, A_HEADS, A_V), lambda ti, bi: (bi, ti, 0, 0)) if av4(i) else tok(n)
                   for i, n in enumerate(widths)],
        out_shape=[jax.ShapeDtypeStruct((b, t, A_HEADS, A_V) if av4(i) else (b, t, n), d)
                   for i, (n, d) in enumerate(zip(widths, dts))],
        compiler_params=_params(("arbitrary", "arbitrary")),
        name="proj0",
    )(x, g, w, *tabs)


def _proj1_kernel(x_ref, g_ref, w_ref, gcq_ref, wuq_ref, gckv_ref, rc_ref, rs1_ref, rs2_ref,
                  qc_ref, clat_ref, kr_ref, dq_ref, dk_ref, dv_ref, sg_ref):
    u = _rms(x_ref[0], g_ref[...]).astype(BF16)
    c, s1, s2 = rc_ref[...], rs1_ref[...], rs2_ref[...]
    cq = _rms(_dot(u, w_ref[:, 0:768]), gcq_ref[...]).astype(BF16)
    clat_ref[0] = _rms(_dot(u, w_ref[:, 768:1024]), gckv_ref[...])
    dq_ref[0] = (_dot(u, w_ref[:, 1024:1536]) * (D_HD ** -0.5)).astype(BF16)
    dk_ref[0] = _dot(u, w_ref[:, 1536:2048])
    dv_ref[0] = _dot(u, w_ref[:, 2048:2560])
    sg_ref[0] = _silu(_dot(u, w_ref[:, 2560:3584])).astype(BF16)
    kr_ref[0] = _rope(_dot(u, w_ref[:, 3584:3712]), c, s1, s2, C_ROPE // 2)
    qc = _dot(cq, wuq_ref[...])
    for h in range(C_HEADS):
        sl = slice(h * LANES, (h + 1) * LANES)
        qc_ref[0, :, sl] = (_rope(qc[:, sl], c, s1, s2, C_ROPE // 2)
                            * ((C_NOPE + C_ROPE) ** -0.5 * LOG2E)).astype(BF16)


def _proj1(x, g, w, gcq, wuq, gckv, tabs, tm):
    b, t, _ = x.shape
    tok = lambda n: pl.BlockSpec((1, tm, n), lambda ti, bi: (bi, ti, 0))
    const = lambda shape: pl.BlockSpec(shape, lambda ti, bi: (0,) * len(shape))
    tab = pl.BlockSpec((tm, LANES), lambda ti, bi: (ti, 0))
    widths = (1024, 256, 128, 512, 512, 512, 1024)
    dts = (BF16, F32, F32, BF16, F32, F32, BF16)
    return pl.pallas_call(
        _proj1_kernel,
        grid=(t // tm, b),
        in_specs=[tok(D_MODEL), const((1, D_MODEL)), const(w.shape), const((1, C_QL)),
                  const(wuq.shape), const((1, C_KVL)), tab, tab, tab],
        out_specs=[tok(n) for n in widths],
        out_shape=[jax.ShapeDtypeStruct((b, t, n), d) for n, d in zip(widths, dts)],
        compiler_params=_params(("arbitrary", "arbitrary")),
        name="proj1",
    )(x, g, w, gcq, wuq, gckv, *tabs)


def _outproj_kernel(o1_ref, o2_ref, w_ref, g_ref, h_ref, out_ref):
    y = _dot(o1_ref[...], w_ref[0:512, :]) + _dot(o2_ref[...], w_ref[512:1024, :])
    out_ref[...] = h_ref[...] + _rms(y, g_ref[...])


def _outproj(o1, o2, w, g, h, tm):
    n = h.shape[0]
    row = lambda c: pl.BlockSpec((tm, c), lambda i: (i, 0))
    const = lambda shape: pl.BlockSpec(shape, lambda i: (0,) * len(shape))
    return pl.pallas_call(
        _outproj_kernel,
        grid=(n // tm,),
        in_specs=[row(512), row(512), const(w.shape), const((1, D_MODEL)), row(D_MODEL)],
        out_specs=row(D_MODEL),
        out_shape=jax.ShapeDtypeStruct((n, D_MODEL), F32),
        compiler_params=_params(("arbitrary",)),
        name="outproj",
    )(o1, o2, w, g, h)


def _bias_lookup(rb_ref, h, idx):
    def body(r, acc):
        return jnp.where(idx == r, rb_ref[h, r], acc)

    return lax.fori_loop(0, 2 * B_CLIP + 1, body, jnp.zeros(idx.shape, F32)) * LOG2E


def _bias_kernel(rb_ref, out_t_ref, out_s_ref, *, st):
    h = pl.program_id(0)
    j = lax.broadcasted_iota(jnp.int32, (WIN_B, TQ_B), 0)
    i = lax.broadcasted_iota(jnp.int32, (WIN_B, TQ_B), 1)
    cj = lax.shift_right_logical(j, 6)
    ci = lax.shift_right_logical(i, 6)
    band = jnp.logical_and(cj >= ci, cj <= ci + B_LEFT)
    idx = jnp.clip(B_BAND + i - j, -B_CLIP, B_CLIP) + B_CLIP
    out_t_ref[0] = jnp.where(band, _bias_lookup(rb_ref, h, idx), NEG)
    i = lax.broadcasted_iota(jnp.int32, (st, WIN_B), 0)
    j = lax.broadcasted_iota(jnp.int32, (st, WIN_B), 1)
    out_s_ref[0] = _bias_lookup(rb_ref, h, jnp.clip(B_BAND + i - j, -B_CLIP, B_CLIP) + B_CLIP)


def _bias_tiles(rel_bias, st):
    return pl.pallas_call(
        functools.partial(_bias_kernel, st=st),
        grid=(B_HEADS,),
        in_specs=[pl.BlockSpec(memory_space=pltpu.SMEM)],
        out_specs=[pl.BlockSpec((1, WIN_B, TQ_B), lambda h: (h // 2, 0, h % 2)),
                   pl.BlockSpec((1, st, WIN_B), lambda h: (h, 0, 0))],
        out_shape=[jax.ShapeDtypeStruct((B_HEADS // 2, WIN_B, 2 * TQ_B), F32),
                   jax.ShapeDtypeStruct((B_HEADS, st, WIN_B), F32)],
        compiler_params=_params(("arbitrary",)),
        name="bias_tile",
    )(rel_bias)


def _ones_rows(n):
    r = lax.broadcasted_iota(jnp.int32, (V_ROWS - LANES, n), 0)
    return jnp.where(r == 0, 1.0, 0.0).astype(BF16)


def _causal_attend_all(n_tiles, n_chains, tq, q_tile, k_rows, vt_cols, finish):
    half = tq // 2
    steps = []
    for qi in range(n_tiles):
        steps += [(qi, j * tq, tq, 0, False) for j in range(qi)]
        steps += [(qi, qi * tq, half, 0, True), (qi, qi * tq + half, half, half, True)]
    qcache = {}

    def scores(step):
        qi, k0, nk, q0, _ = step
        for c in range(n_chains):
            if (c, qi) not in qcache:
                qcache[(c, qi)] = q_tile(c, qi)
        return [_dot_nt(k_rows(c, k0, nk), qcache[(c, qi)][q0:tq, :]) for c in range(n_chains)]

    def chunk_mask(step):
        _, k0, nk, q0, _ = step
        j = lax.broadcasted_iota(jnp.int32, (nk, tq - q0), 0) + (k0 % tq)
        i = lax.broadcasted_iota(jnp.int32, (nk, tq - q0), 1) + q0
        return lax.shift_right_logical(j, 6) <= lax.shift_right_logical(i, 6)

    nxt = scores(steps[0])
    m, acc = [None] * n_chains, [None] * n_chains
    for idx, step in enumerate(steps):
        qi, k0, nk, q0, masked = step
        sts = nxt
        if idx + 1 < len(steps):
            nxt = scores(steps[idx + 1])
        if masked:
            msk = chunk_mask(step)
            sts = [jnp.where(msk, st, NEG) for st in sts]
        first = idx == 0 or steps[idx - 1][0] != qi
        pts, alphas = [], []
        for c in range(n_chains):
            top = jnp.max(sts[c], axis=0, keepdims=True)
            if not first:
                old = m[c][:, q0:tq]
                top = jnp.maximum(old, top)
                alphas.append(jnp.exp2(old - top))
                m[c] = top if q0 == 0 else jnp.concatenate([m[c][:, 0:q0], top], axis=1)
            else:
                m[c] = top
            pts.append(jnp.exp2(sts[c] - top).astype(BF16))
        vt = vt_cols(k0, nk)
        for c in range(n_chains):
            pv = _dot(vt, pts[c])
            if first:
                acc[c] = pv
            elif q0 == 0:
                acc[c] = alphas[c] * acc[c] + pv
            else:
                acc[c] = jnp.concatenate([acc[c][:, 0:q0], alphas[c] * acc[c][:, q0:tq] + pv], axis=1)
        if idx + 1 == len(steps) or steps[idx + 1][0] != qi:
            finish(qi, [a[0:LANES, :] * (1.0 / a[LANES:LANES + 1, :]) for a in acc])


def _lam(lq1, lk1, lq2, lk2, lam_init):
    return (jnp.exp(jnp.sum(lq1[...] * lk1[...], axis=1, keepdims=True))
            - jnp.exp(jnp.sum(lq2[...] * lk2[...], axis=1, keepdims=True)) + lam_init)


def _head_split(q):
    lo = _lane_lo()
    zero = jnp.zeros_like(q)
    return [jnp.where(lo, q, zero), jnp.where(lo, zero, q)]


def _attn_a_kernel(q_ref, k_ref, v_ref, lq1, lk1, lq2, lk2, gsub_ref, sg_ref, o_ref, kb_ref, vt_ref,
                   *, tq, t, lam_init):
    kb_ref[...] = k_ref[0].astype(BF16)
    vt_ref[0:LANES, :] = v_ref[0].T.astype(BF16)
    vt_ref[LANES:V_ROWS, :] = _ones_rows(t)
    lam = _lam(lq1, lk1, lq2, lk2, lam_init)
    tile = lambda i: slice(i * tq, (i + 1) * tq)
    split = {}

    def q_tile(c, qi):
        if qi not in split:
            split[qi] = _head_split(q_ref[0, tile(qi), :])
        return split[qi][c]

    def finish(qi, outs):
        o = outs[0].T - lam * outs[1].T
        o = _rms(o, gsub_ref[...]) * (1.0 - lam_init)
        o_ref[0, tile(qi), :] = (o * sg_ref[0, tile(qi), :].astype(F32)).astype(BF16)

    _causal_attend_all(t // tq, 2, tq, q_tile, lambda c, k0, n: kb_ref[k0:k0 + n, :],
                       lambda k0, n: vt_ref[:, k0:k0 + n], finish)


def _attn_a(q, k, v, lams, gsub, sg, lam_init):
    b, t, _ = q.shape
    tq = min(TQ, t)
    blk = pl.BlockSpec((1, t, LANES), lambda bi, h: (bi, 0, h))
    vec = lambda n: pl.BlockSpec((1, n), lambda bi, h: (0, 0))
    return pl.pallas_call(
        functools.partial(_attn_a_kernel, tq=tq, t=t, lam_init=lam_init),
        grid=(b, A_HEADS),
        in_specs=[blk, blk, blk, vec(A_QK), vec(A_QK), vec(A_QK), vec(A_QK), vec(A_V), blk],
        out_specs=blk,
        out_shape=jax.ShapeDtypeStruct((b, t, A_HEADS * A_V), BF16),
        scratch_shapes=[pltpu.VMEM((t, LANES), BF16), pltpu.VMEM((V_ROWS, t), BF16)],
        compiler_params=_params(("arbitrary", "arbitrary")),
        name="attn_a",
    )(q, k, v, *lams, gsub, sg)


def _attn_b_kernel(q_ref, k_ref, v_ref, bias_ref, sg_ref, o_ref, kb_ref, vt_ref, *, t):
    kb_ref[0:B_BAND, :] = jnp.zeros((B_BAND, LANES), BF16)
    kb_ref[B_BAND:B_BAND + t, :] = k_ref[0].astype(BF16)
    vt_ref[:, 0:B_BAND] = jnp.zeros((V_ROWS, B_BAND), BF16)
    vt_ref[0:LANES, B_BAND:B_BAND + t] = v_ref[0].T.astype(BF16)
    vt_ref[LANES:V_ROWS, B_BAND:B_BAND + t] = _ones_rows(t)
    key = lax.broadcasted_iota(jnp.int32, (WIN_B, 1), 0)
    row_lo = _row_lo()

    def tiles(q0s, first_valids):
        def scores(n):
            qa, qb = _head_split(q_ref[0, pl.ds(q0s[n], TQ_B), :])
            st = _dot_nt(kb_ref[pl.ds(q0s[n], WIN_B), :], jnp.concatenate([qa, qb], axis=0)) + bias_ref[0]
            if first_valids[n] > 0:
                st = jnp.where(key < first_valids[n], NEG, st)
            return st

        nxt = scores(0)
        for n in range(len(q0s)):
            st = nxt
            if n + 1 < len(q0s):
                nxt = scores(n + 1)
            pt = jnp.exp2(st - jnp.max(st, axis=0, keepdims=True)).astype(BF16)
            acc = _dot(vt_ref[:, pl.ds(q0s[n], WIN_B)], pt)
            inv = 1.0 / acc[LANES:LANES + 1, :]
            ot = jnp.where(row_lo, acc[0:LANES, 0:TQ_B] * inv[:, 0:TQ_B],
                           acc[0:LANES, TQ_B:2 * TQ_B] * inv[:, TQ_B:2 * TQ_B])
            rows = pl.ds(q0s[n], TQ_B)
            o_ref[0, rows, :] = (ot.T * sg_ref[0, rows, :].astype(F32)).astype(BF16)

    n_edge = min(B_BAND, t) // TQ_B
    tiles([qi * TQ_B for qi in range(n_edge)], [B_BAND - qi * TQ_B for qi in range(n_edge)])
    per_iter = 4

    def body(i, carry):
        q0 = pl.multiple_of(n_edge * TQ_B + i * per_iter * TQ_B, per_iter * TQ_B)
        tiles([q0 + n * TQ_B for n in range(per_iter)], [0] * per_iter)
        return carry

    lax.fori_loop(0, (t - n_edge * TQ_B) // (per_iter * TQ_B), body, 0)


def _attn_b(q, k, v, bias_t, sg):
    b, t, _ = q.shape
    assert t % (4 * TQ_B) == 0
    blk = pl.BlockSpec((1, t, LANES), lambda bi, h: (bi, 0, h))
    sgblk = pl.BlockSpec((1, t, LANES), lambda bi, h: (bi, 0, h + A_HEADS))
    bblk = pl.BlockSpec((1, WIN_B, 2 * TQ_B), lambda bi, h: (h, 0, 0))
    return pl.pallas_call(
        functools.partial(_attn_b_kernel, t=t),
        grid=(b, B_HEADS // 2),
        in_specs=[blk, blk, blk, bblk, sgblk],
        out_specs=blk,
        out_shape=jax.ShapeDtypeStruct((b, t, B_HEADS * B_HD), BF16),
        scratch_shapes=[pltpu.VMEM((B_BAND + t, LANES), BF16), pltpu.VMEM((V_ROWS, B_BAND + t), BF16)],
        compiler_params=_params(("arbitrary", "arbitrary")),
        name="attn_b",
    )(q, k, v, bias_t, sg)


def _attn_c_kernel(q_ref, clat_ref, kr_ref, wuk_ref, wuv_ref, sg_ref, o_ref, kc_ref, vt_ref, *, tq, t):
    cl = clat_ref[0].astype(BF16)
    kr = kr_ref[0]
    for hh in range(2):
        kc_ref[hh] = (_dot(cl, wuk_ref[hh]) + kr).astype(BF16)
    vt_ref[0:LANES, :] = _dot(cl, wuv_ref[0]).T.astype(BF16)
    vt_ref[LANES:V_ROWS, :] = _ones_rows(t)
    tile = lambda i: slice(i * tq, (i + 1) * tq)
    row_lo = _row_lo()

    def finish(qi, outs):
        o = jnp.where(row_lo, outs[0], outs[1]).T
        o_ref[0, tile(qi), :] = (o * sg_ref[0, tile(qi), :].astype(F32)).astype(BF16)

    _causal_attend_all(t // tq, 2, tq, lambda c, qi: q_ref[0, tile(qi), c * LANES:(c + 1) * LANES],
                       lambda c, k0, n: kc_ref[c, k0:k0 + n, :], lambda k0, n: vt_ref[:, k0:k0 + n], finish)


def _attn_c(qc, clat, kr, wuk, wuv, sg):
    b, t, _ = qc.shape
    tq = min(TQ, t)
    return pl.pallas_call(
        functools.partial(_attn_c_kernel, tq=tq, t=t),
        grid=(b, C_HEADS // 2),
        in_specs=[pl.BlockSpec((1, t, 2 * LANES), lambda bi, h: (bi, 0, h)),
                  pl.BlockSpec((1, t, C_KVL), lambda bi, h: (bi, 0, 0)),
                  pl.BlockSpec((1, t, LANES), lambda bi, h: (bi, 0, 0)),
                  pl.BlockSpec((2, C_KVL, LANES), lambda bi, h: (h, 0, 0)),
                  pl.BlockSpec((1, C_KVL, LANES), lambda bi, h: (h, 0, 0)),
                  pl.BlockSpec((1, t, LANES), lambda bi, h: (bi, 0, h))],
        out_specs=pl.BlockSpec((1, t, LANES), lambda bi, h: (bi, 0, h)),
        out_shape=jax.ShapeDtypeStruct((b, t, C_HEADS * C_V), BF16),
        scratch_shapes=[pltpu.VMEM((2, t, LANES), BF16), pltpu.VMEM((V_ROWS, t), BF16)],
        compiler_params=_params(("arbitrary", "arbitrary")),
        name="attn_c",
    )(qc, clat, kr, wuk, wuv, sg)


def _suffix_sums_t(msp, negut, tk):
    mb = msp.astype(BF16)
    if msp.shape[0] == tk:
        return _dot(negut, mb)
    late = _dot(negut, mb[tk:2 * tk, :])
    return jnp.concatenate([_dot(negut, mb[0:tk, :]) + late[0:1, :], late], axis=0)


def _stick_units_t(qs, ks, vts, negut, runs, tri_t, tk):
    n = len(qs)

    def diag_only(x):
        return x if tri_t is None else jnp.concatenate([x[0:tk, :], jnp.where(tri_t, x[tk:2 * tk, :], 0.0)], axis=0)

    zts, stage1, out = [None] * n, [None] * n, [None] * n
    for step in range(n + 2):
        if step < n:
            zts[step] = _dot_nt(ks[step], qs[step])
        c = step - 1
        if 0 <= c < n:
            later = _suffix_sums_t(diag_only(_softplus(zts[c])), negut, tk)
            stage1[c] = (zts[c] + later, later[0:1, :])
        c = step - 2
        if 0 <= c < n:
            x, total = stage1[c]
            if runs is not None:
                x = x + runs[c]
            out[c] = (total, _dot(vts[c], diag_only(jnp.exp(x)).astype(BF16)))
    return out


def _attn_d_kernel(q_ref, k_ref, v_ref, negut_ref, sg_ref, o_ref, kb_ref, vt_ref, *, tk, t, nq):
    qt = pl.program_id(2)

    @pl.when(qt == 0)
    def _():
        kb_ref[0:tk, :] = jnp.zeros((tk, LANES), BF16)
        vt_ref[:, 0:tk] = jnp.zeros((LANES, tk), BF16)
        kb_ref[tk:tk + t, :] = k_ref[0].astype(BF16)
        vt_ref[:, tk:tk + t] = v_ref[0].T.astype(BF16)

    qs, tile_of = [], []
    for h in range(nq):
        qs += _head_split(q_ref[0, h * tk:(h + 1) * tk, :])
        tile_of += [h, h]
    j = lax.broadcasted_iota(jnp.int32, (tk, tk), 0)
    i = lax.broadcasted_iota(jnp.int32, (tk, tk), 1)
    tri_t = j < i
    wins = [pl.ds(pl.multiple_of((nq * qt + h) * tk, tk), 2 * tk) for h in tile_of]
    first = _stick_units_t(qs, [kb_ref[w, :] for w in wins], [vt_ref[:, w] for w in wins], negut_ref[...],
                           None, tri_t, tk)
    runs = [f[0] for f in first]
    accs = [f[1] for f in first]

    def top(rs):
        return functools.reduce(jnp.maximum, [jnp.max(r) for r in rs])

    def cond(carry):
        return jnp.logical_and(carry[0] < nq * qt + nq - 2, carry[1] > -104.0)

    def body(carry):
        n, _, runs, accs = carry
        tiles = [pl.ds(pl.multiple_of(jnp.maximum(nq * qt + h - 1 - n, 0) * tk, tk), tk) for h in tile_of]
        res = _stick_units_t(qs, [kb_ref[w, :] for w in tiles], [vt_ref[:, w] for w in tiles],
                             negut_ref[...], runs, None, tk)
        runs = [r + d for r, (d, _) in zip(runs, res)]
        accs = [a + c for a, (_, c) in zip(accs, res)]
        return n + 1, top(runs), runs, accs

    _, _, _, accs = lax.while_loop(cond, body, (jnp.int32(0), top(runs), runs, accs))
    row_lo = _row_lo()
    for h in range(nq):
        ot = jnp.where(row_lo, accs[2 * h], accs[2 * h + 1])
        rows = slice(h * tk, (h + 1) * tk)
        o_ref[0, rows, :] = (ot.T * sg_ref[0, rows, :].astype(F32)).astype(BF16)


def _attn_d(q, k, v, negut, sg):
    b, t, _ = q.shape
    tk = min(TQ_D, t // NQ_D)
    qblk = pl.BlockSpec((1, NQ_D * tk, LANES), lambda bi, h, qi: (bi, qi, h))
    sgblk = pl.BlockSpec((1, NQ_D * tk, LANES), lambda bi, h, qi: (bi, qi, h + C_HEADS // 2))
    kvblk = pl.BlockSpec((1, t, LANES), lambda bi, h, qi: (bi, 0, h))
    ublk = pl.BlockSpec((tk, tk), lambda bi, h, qi: (0, 0))
    return pl.pallas_call(
        functools.partial(_attn_d_kernel, tk=tk, t=t, nq=NQ_D),
        grid=(b, D_HEADS // 2, t // (NQ_D * tk)),
        in_specs=[qblk, kvblk, kvblk, ublk, sgblk],
        out_specs=qblk,
        out_shape=jax.ShapeDtypeStruct((b, t, D_HEADS * D_HD), BF16),
        scratch_shapes=[pltpu.VMEM((tk + t, LANES), BF16), pltpu.VMEM((LANES, tk + t), BF16)],
        compiler_params=_params(("arbitrary", "arbitrary", "arbitrary")),
        name="attn_d",
    )(q, k, v, negut, sg)


def _two_part_softmax(q, kct, kn, vc, vn, bc=None, bn=None, vc_t=False):
    sc = _dot(q, kct)
    sn = _dot_nt(q, kn)
    if bc is not None:
        sc = sc + bc
        sn = sn + bn
    m = jnp.maximum(jnp.max(sc, axis=1, keepdims=True), jnp.max(sn, axis=1, keepdims=True))
    pc = jnp.exp2(sc - m)
    pn = jnp.exp2(sn - m)
    l = jnp.sum(pc, axis=1, keepdims=True) + jnp.sum(pn, axis=1, keepdims=True)
    pvc = _dot_nt(pc.astype(BF16), vc) if vc_t else _dot(pc.astype(BF16), vc)
    return (pvc + _dot(pn.astype(BF16), vn)) * (1.0 / l)


def _samp_a_kernel(q_ref, ck_ref, cv_ref, nk_ref, nv_ref, lq1, lk1, lq2, lk2, gsub_ref, sg_ref, o_ref,
                   *, lam_init):
    lam = _lam(lq1, lk1, lq2, lk2, lam_init)
    for h in range(A_HEADS):
        sl = slice(h * LANES, (h + 1) * LANES)
        kct, vc = ck_ref[0, h].astype(BF16), cv_ref[0, :, h, :].astype(BF16)
        kn, vn = nk_ref[0, :, sl].astype(BF16), nv_ref[0, :, sl].astype(BF16)
        q0, q1 = _head_split(q_ref[0, :, sl])
        o0 = _two_part_softmax(q0, kct, kn, vc, vn)
        o1 = _two_part_softmax(q1, kct, kn, vc, vn)
        o = _rms(o0 - lam * o1, gsub_ref[...]) * (1.0 - lam_init)
        o_ref[0, :, sl] = (o * sg_ref[0, :, sl].astype(F32)).astype(BF16)


def _samp_b_kernel(q_ref, ck_ref, cv_ref, nk_ref, nv_ref, bias_ref, sg_ref, o_ref, *, band, st):
    lo = _lane_lo()
    for hp in range(B_HEADS // 2):
        sl = slice(hp * LANES, (hp + 1) * LANES)
        kct, vct = ck_ref[0, hp].astype(BF16), cv_ref[0, hp].astype(BF16)
        kn, vn = nk_ref[0, :, sl].astype(BF16), nv_ref[0, :, sl].astype(BF16)
        outs = []
        for hh, qm in enumerate(_head_split(q_ref[0, :, sl])):
            bias = bias_ref[2 * hp + hh]
            outs.append(_two_part_softmax(qm, kct, kn, vct, vn, bias[:, 0:band], bias[:, band:band + st],
                                          vc_t=True))
        o = jnp.where(lo, outs[0], outs[1])
        o_ref[0, :, sl] = (o * sg_ref[0, :, sl].astype(F32)).astype(BF16)


def _samp_c_kernel(q_ref, clat_ref, krp_ref, nclat_ref, nkr_ref, wukt_ref, wuv_ref, sg_ref, o_ref, *, st):
    lo = _lane_lo()
    cl = clat_ref[0].astype(BF16)
    ncl = nclat_ref[0].astype(BF16)
    past = krp_ref.shape[2]
    krct = jnp.concatenate([jnp.zeros((C_NOPE, past), BF16), krp_ref[0].astype(BF16),
                            jnp.zeros((LANES - C_NOPE - C_ROPE, past), BF16)], axis=0)
    krn = nkr_ref[0].astype(BF16)
    qh = [q_ref[0, :, h * LANES:(h + 1) * LANES] for h in range(C_HEADS)]
    q_all = jnp.concatenate(qh, axis=0)
    q_lat = jnp.concatenate([_dot(qh[h], wukt_ref[h]) for h in range(C_HEADS)], axis=0).astype(BF16)
    sc = _dot_nt(q_lat, cl) + _dot(q_all, krct)
    sn = _dot_nt(q_lat, ncl) + _dot_nt(q_all, krn)
    m = jnp.maximum(jnp.max(sc, axis=1, keepdims=True), jnp.max(sn, axis=1, keepdims=True))
    pc = jnp.exp2(sc - m)
    pn = jnp.exp2(sn - m)
    l = jnp.sum(pc, axis=1, keepdims=True) + jnp.sum(pn, axis=1, keepdims=True)
    o_lat = ((_dot(pc.astype(BF16), cl) + _dot(pn.astype(BF16), ncl)) * (1.0 / l)).astype(BF16)
    for hp in range(C_HEADS // 2):
        sl = slice(hp * LANES, (hp + 1) * LANES)
        oa = _dot(o_lat[(2 * hp) * st:(2 * hp + 1) * st, :], wuv_ref[hp])
        ob = _dot(o_lat[(2 * hp + 1) * st:(2 * hp + 2) * st, :], wuv_ref[hp])
        o_ref[0, :, sl] = (jnp.where(lo, oa, ob) * sg_ref[0, :, sl].astype(F32)).astype(BF16)


def _stick_unit(q, k, v, negu, run, tri, kv_t=False):
    z = _dot(q, k) if kv_t else _dot_nt(q, k)
    sp = _softplus(z)
    msp = sp if tri is None else jnp.where(tri, sp, 0.0)
    later = _dot(msp.astype(BF16), negu)
    x = z - sp + later
    if run is not None:
        x = x + run
    a = jnp.exp(x)
    if tri is not None:
        a = jnp.where(tri, a, 0.0)
    total = later[:, 0:1] - msp[:, 0:1]
    return total, (_dot_nt(a.astype(BF16), v) if kv_t else _dot(a.astype(BF16), v))


def _neg_suffix(n):
    r = lax.broadcasted_iota(jnp.int32, (n, n), 0)
    c = lax.broadcasted_iota(jnp.int32, (n, n), 1)
    return jnp.where(r > c, -1.0, 0.0).astype(BF16)


def _samp_d_kernel(q_ref, ck_ref, cv_ref, nk_ref, nv_ref, sg_ref, o_ref, *, past, st, tk):
    lo = _lane_lo()
    negu_new = _neg_suffix(st)
    r = lax.broadcasted_iota(jnp.int32, (st, st), 0)
    c = lax.broadcasted_iota(jnp.int32, (st, st), 1)
    tri = c < r
    negu = _neg_suffix(tk)
    n_chunks = past // tk
    for hp in range(D_HEADS // 2):
        sl = slice(hp * LANES, (hp + 1) * LANES)
        kn, vn = nk_ref[0, :, sl].astype(BF16), nv_ref[0, :, sl].astype(BF16)
        qa, qb = _head_split(q_ref[0, :, sl])
        run_a, acc_a = _stick_unit(qa, kn, vn, negu_new, None, tri)
        run_b, acc_b = _stick_unit(qb, kn, vn, negu_new, None, tri)

        def cond(carry):
            return jnp.logical_and(carry[0] < n_chunks, carry[1] > -104.0)

        def body(carry, qa=qa, qb=qb, hp=hp):
            n, _, run_a, acc_a, run_b, acc_b = carry
            keys = pl.ds(pl.multiple_of((n_chunks - 1 - n) * tk, tk), tk)
            kct, vct = ck_ref[0, hp, :, keys].astype(BF16), cv_ref[0, hp, :, keys].astype(BF16)
            da, ca = _stick_unit(qa, kct, vct, negu, run_a, None, kv_t=True)
            db, cb = _stick_unit(qb, kct, vct, negu, run_b, None, kv_t=True)
            run_a, run_b = run_a + da, run_b + db
            return n + 1, jnp.maximum(jnp.max(run_a), jnp.max(run_b)), run_a, acc_a + ca, run_b, acc_b + cb

        init = (jnp.int32(0), jnp.maximum(jnp.max(run_a), jnp.max(run_b)), run_a, acc_a, run_b, acc_b)
        _, _, _, acc_a, _, acc_b = lax.while_loop(cond, body, init)
        o = jnp.where(lo, acc_a, acc_b)
        o_ref[0, :, sl] = (o * sg_ref[0, :, sl].astype(F32)).astype(BF16)


def _samp_call(body, name, b, st, ins, specs):
    return pl.pallas_call(
        body,
        grid=(b,),
        in_specs=specs,
        out_specs=pl.BlockSpec((1, st, 512), lambda bi: (bi, 0, 0)),
        out_shape=jax.ShapeDtypeStruct((b, st, 512), BF16),
        compiler_params=_params(("arbitrary",)),
        name=name,
    )(*ins)


def _per_batch(shape, col=0):
    return pl.BlockSpec((1,) + tuple(shape[1:]), lambda bi: (bi,) + (0,) * (len(shape) - 2) + (col,))


def _whole(shape):
    return pl.BlockSpec(tuple(shape), lambda bi: (0,) * len(shape))


def _rope_tables(pos, rot, period, offset):
    half = rot // 2
    inv_freq = ROPE_THETA ** (-jnp.arange(half, dtype=F32) / half)
    ang = pos.astype(F32)[:, None] * inv_freq[None, :]
    cos, sin = jnp.cos(ang), jnp.sin(ang)
    n = pos.shape[0]
    c = jnp.ones((n, period), F32).at[:, offset:offset + half].set(cos).at[:, offset + half:offset + rot].set(cos)
    s1 = jnp.zeros((n, period), F32).at[:, offset:offset + half].set(-sin)
    s2 = jnp.zeros((n, period), F32).at[:, offset + half:offset + rot].set(sin)
    reps = LANES // period
    return tuple(jnp.tile(x, (1, reps)) for x in (c, s1, s2))


def kernel(x_prompt, x_sample, cache_a_k, cache_a_v, cache_b_k, cache_b_v, cache_c_latent, cache_c_krope,
           cache_d_k, cache_d_v, g_pre0, w_in0, lam_q1, lam_k1, lam_q2, lam_k2, g_sub_a, rel_bias_b, w_out0,
           g_post0, g_pre1, w_in1, g_cq, w_uq, g_ckv, w_uk, w_uv, w_out1, g_post1):
    b, t, _ = x_prompt.shape
    sb, st, _ = x_sample.shape
    past = cache_a_k.shape[1]
    band = cache_b_k.shape[1]
    assert band == B_BAND and t % TQ == 0 and past % TQ_D == 0
    ns = sb * st
    row = lambda x: x.reshape(1, -1)

    w0 = w_in0.astype(BF16)
    wo0 = w_out0.astype(BF16)
    wo1 = w_out1.astype(BF16)
    w1 = jnp.concatenate([w_in1[:, 0:1024], w_in1[:, 1056:3616], jnp.zeros((D_MODEL, 64), F32),
                          w_in1[:, 1024:1056], jnp.zeros((D_MODEL, 32), F32)], axis=1).astype(BF16)
    wuq = jnp.pad(w_uq.reshape(C_QL, C_HEADS, C_NOPE + C_ROPE), ((0, 0), (0, 0), (0, 32)))
    wuq = wuq.reshape(C_QL, C_HEADS * LANES).astype(BF16)
    wuk = jnp.pad(w_uk.transpose(1, 0, 2), ((0, 0), (0, 0), (0, LANES - C_NOPE))).astype(BF16)
    wuv = w_uv.reshape(C_KVL, C_HEADS // 2, 2 * C_V).transpose(1, 0, 2).astype(BF16)
    n_u = min(TQ_D, t // NQ_D)
    negut = -jnp.triu(jnp.ones((n_u, n_u), BF16))
    lams = tuple(row(x) for x in (lam_q1, lam_k1, lam_q2, lam_k2))
    lam_init = 0.8 - 0.6 * math.exp(-0.3 * 0)

    pos_p = jnp.arange(t)
    pos_s = past + jnp.arange(st)
    tabs0_p = _rope_tables(pos_p, A_ROT, A_QK, 0)
    tabs0_s = tuple(jnp.tile(x, (sb, 1)) for x in _rope_tables(pos_s, A_ROT, A_QK, 0))
    tabs1_p = _rope_tables(pos_p, C_ROPE, LANES, C_NOPE)
    tabs1_s = tuple(jnp.tile(x, (sb, 1)) for x in _rope_tables(pos_s, C_ROPE, LANES, C_NOPE))

    bias_t, bias_s = _bias_tiles(rel_bias_b, st)

    aq, ak, av, bq, bk, bv, sg0, avb = _proj0(x_prompt, row(g_pre0), w0, tabs0_p, 512)
    oa = _attn_a(aq, ak, avb, lams, row(g_sub_a), sg0, lam_init)
    ob = _attn_b(bq, bk, bv, bias_t, sg0)
    h1 = _outproj(oa.reshape(b * t, 512), ob.reshape(b * t, 512), wo0, row(g_post0),
                  x_prompt.reshape(b * t, D_MODEL), 1024)

    def keys_minor(x):
        n = x.shape[1]
        return jnp.moveaxis(x, 1, -1).reshape(sb, -1, LANES, n)

    xs = x_sample.reshape(1, ns, D_MODEL)
    s_out = _proj0(xs, row(g_pre0), w0, tabs0_s, ns)
    aq_s, ak_s, av_s, bq_s, bk_s, bv_s, sg0_s, _ = (x.reshape(sb, st, -1) for x in s_out)
    vec = lambda n: _whole((1, n))
    oa_s = _samp_call(
        functools.partial(_samp_a_kernel, lam_init=lam_init), "samp_a", sb, st,
        (aq_s, keys_minor(cache_a_k), cache_a_v, ak_s, av_s,
         *lams, row(g_sub_a), sg0_s),
        [_per_batch((sb, st, 512)), _per_batch((sb, A_HEADS, LANES, past)), _per_batch((sb, past, A_HEADS, A_V)),
         _per_batch((sb, st, 512)), _per_batch((sb, st, 512)), vec(A_QK), vec(A_QK), vec(A_QK), vec(A_QK),
         vec(A_V), _per_batch((sb, st, 512))])
    ob_s = _samp_call(
        functools.partial(_samp_b_kernel, band=band, st=st), "samp_b", sb, st,
        (bq_s, keys_minor(cache_b_k), keys_minor(cache_b_v), bk_s, bv_s, bias_s, sg0_s),
        [_per_batch((sb, st, 512)), _per_batch((sb, B_HEADS // 2, LANES, band)),
         _per_batch((sb, B_HEADS // 2, LANES, band)),
         _per_batch((sb, st, 512)), _per_batch((sb, st, 512)),
         pl.BlockSpec((B_HEADS, st, WIN_B), lambda bi: (0, 0, 0)), _per_batch((sb, st, 512), col=1)])
    hs1 = _outproj(oa_s.reshape(ns, 512), ob_s.reshape(ns, 512), wo0, row(g_post0),
                   x_sample.reshape(ns, D_MODEL), ns)

    qc, clat, kr, dq, dk, dv, sg1 = _proj1(h1.reshape(b, t, D_MODEL), row(g_pre1), w1, row(g_cq), wuq,
                                           row(g_ckv), tabs1_p, 512)
    oc = _attn_c(qc, clat, kr, wuk, wuv, sg1)
    od = _attn_d(dq, dk, dv, negut, sg1)
    h2 = _outproj(oc.reshape(b * t, 512), od.reshape(b * t, 512), wo1, row(g_post1), h1, 1024)

    s_out = _proj1(hs1.reshape(1, ns, D_MODEL), row(g_pre1), w1, row(g_cq), wuq, row(g_ckv), tabs1_s, ns)
    qc_s, clat_s, kr_s, dq_s, dk_s, dv_s, sg1_s = (x.reshape(sb, st, -1) for x in s_out)
    krt = jnp.moveaxis(cache_c_krope, 1, -1)
    wukt = wuk.transpose(0, 2, 1)
    oc_s = _samp_call(
        functools.partial(_samp_c_kernel, st=st), "samp_c", sb, st,
        (qc_s, cache_c_latent, krt, clat_s, kr_s, wukt, wuv, sg1_s),
        [_per_batch((sb, st, 1024)), _per_batch((sb, past, C_KVL)), _per_batch((sb, C_ROPE, past)),
         _per_batch((sb, st, C_KVL)), _per_batch((sb, st, LANES)), _whole(wukt.shape), _whole(wuv.shape),
         _per_batch((sb, st, 512))])
    od_s = _samp_call(
        functools.partial(_samp_d_kernel, past=past, st=st, tk=TQ_D), "samp_d", sb, st,
        (dq_s, keys_minor(cache_d_k), keys_minor(cache_d_v), dk_s, dv_s, sg1_s),
        [_per_batch((sb, st, 512)), _per_batch((sb, D_HEADS // 2, LANES, past)),
         _per_batch((sb, D_HEADS // 2, LANES, past)),
         _per_batch((sb, st, 512)), _per_batch((sb, st, 512)), _per_batch((sb, st, 512), col=1)])
    hs2 = _outproj(oc_s.reshape(ns, 512), od_s.reshape(ns, 512), wo1, row(g_post1), hs1, ns)

    b_rows = min(B_BAND, t)
    return (h2.reshape(b, t, D_MODEL), hs2.reshape(sb, st, D_MODEL),
            ak.reshape(b, t, A_HEADS, 2, A_QK), av,
            bk[:, t - b_rows:].reshape(b, b_rows, B_HEADS, B_HD), bv[:, t - b_rows:].reshape(b, b_rows, B_HEADS, B_HD),
            clat, kr[:, :, C_NOPE:C_NOPE + C_ROPE],
            dk.reshape(b, t, D_HEADS, D_HD), dv.reshape(b, t, D_HEADS, D_HD),
            ak_s.reshape(sb, st, A_HEADS, 2, A_QK), av_s.reshape(sb, st, A_HEADS, A_V),
            bk_s.reshape(sb, st, B_HEADS, B_HD), bv_s.reshape(sb, st, B_HEADS, B_HD),
            clat_s, kr_s[:, :, C_NOPE:C_NOPE + C_ROPE],
            dk_s.reshape(sb, st, D_HEADS, D_HD), dv_s.reshape(sb, st, D_HEADS, D_HD))
```

```python
import functools
import math

import jax
import jax.numpy as jnp
from jax import lax
from jax.experimental import pallas as pl
from jax.experimental.pallas import tpu as pltpu

F32 = jnp.float32
BF16 = jnp.bfloat16

D_MODEL = 1024
CHUNK = 64
ROPE_THETA = 500000.0
NORM_EPS = 1e-6
A_HEADS, A_QK, A_V, A_ROT = 4, 64, 128, 16
B_HEADS, B_HD, B_LEFT, B_CLIP = 8, 64, 8, 128
C_HEADS, C_NOPE, C_ROPE, C_V, C_QL, C_KVL = 8, 64, 32, 64, 768, 256
D_HEADS, D_HD = 8, 64
B_BAND = B_LEFT * CHUNK
LANES = 128
NEG = -1e30
VMEM_LIMIT = 56 * 1024 * 1024

TQ = 512
TQ_D = 256
NQ_D = 4
TQ_B = 128
WIN_B = B_BAND + TQ_B
LOG2E = 1.4426950408889634
V_ROWS = 144


def _dot(a, b):
    return jnp.dot(a, b, preferred_element_type=F32)


def _dot_nt(a, b):
    return lax.dot_general(a, b, (((1,), (1,)), ((), ())), preferred_element_type=F32)


def _rms(x, g):
    ms = jnp.mean(x * x, axis=-1, keepdims=True)
    return x * lax.rsqrt(ms + NORM_EPS) * g


def _silu(x):
    return x * (1.0 / (1.0 + jnp.exp(-x)))


def _softplus(z):
    return jnp.maximum(z, 0.0) + jnp.log(1.0 + jnp.exp(-jnp.abs(z)))


def _rope(z, c, s1, s2, shift):
    return z * c + pltpu.roll(z, LANES - shift, 1) * s1 + pltpu.roll(z, shift, 1) * s2


def _lane_lo():
    return lax.broadcasted_iota(jnp.int32, (1, LANES), 1) < 64


def _row_lo():
    return lax.broadcasted_iota(jnp.int32, (LANES, 1), 0) < 64


def _params(sem):
    return pltpu.CompilerParams(dimension_semantics=sem, vmem_limit_bytes=VMEM_LIMIT)


def _proj0_kernel(x_ref, g_ref, w_ref, rc_ref, rs1_ref, rs2_ref,
                  aq_ref, ak_ref, av_ref, bq_ref, bk_ref, bv_ref, sg_ref, avb_ref):
    u = _rms(x_ref[0], g_ref[...]).astype(BF16)
    c, s1, s2 = rc_ref[...], rs1_ref[...], rs2_ref[...]
    zq = _dot(u, w_ref[:, 0:512])
    zk = _dot(u, w_ref[:, 512:1024])
    for v in range(4):
        sl = slice(v * LANES, (v + 1) * LANES)
        aq_ref[0, :, sl] = (_rope(zq[:, sl], c, s1, s2, A_ROT // 2) * (A_QK ** -0.5 * LOG2E)).astype(BF16)
        ak_ref[0, :, sl] = _rope(zk[:, sl], c, s1, s2, A_ROT // 2)
    zv = _dot(u, w_ref[:, 1024:1536])
    avb_ref[0] = zv.astype(BF16)
    for h in range(A_HEADS):
        av_ref[0, :, h, :] = zv[:, h * A_V:(h + 1) * A_V]
    bq_ref[0] = (_dot(u, w_ref[:, 1536:2048]) * (B_HD ** -0.5 * LOG2E)).astype(BF16)
    bk_ref[0] = _dot(u, w_ref[:, 2048:2560])
    bv_ref[0] = _dot(u, w_ref[:, 2560:3072])
    sg_ref[0] = _silu(_dot(u, w_ref[:, 3072:4096])).astype(BF16)


def _proj0(x, g, w, tabs, tm):
    b, t, _ = x.shape
    tok = lambda n: pl.BlockSpec((1, tm, n), lambda ti, bi: (bi, ti, 0))
    const = lambda shape: pl.BlockSpec(shape, lambda ti, bi: (0,) * len(shape))
    tab = pl.BlockSpec((tm, LANES), lambda ti, bi: (ti, 0))
    widths = (512, 512, 512, 512, 512, 512, 1024, 512)
    dts = (BF16, F32, F32, BF16, F32, F32, BF16, BF16)
    av4 = lambda i: i == 2
    return pl.pallas_call(
        _proj0_kernel,
        grid=(t // tm, b),
        in_specs=[tok(D_MODEL), const((1, D_MODEL)), const(w.shape), tab, tab, tab],
        out_specs=[pl.BlockSpec((1, tm, A_HEADS, A_V), lambda ti, bi: (bi, ti, 0, 0)) if av4(i) else tok(n)
                   for i, n in enumerate(widths)],
        out_shape=[jax.ShapeDtypeStruct((b, t, A_HEADS, A_V) if av4(i) else (b, t, n), d)
                   for i, (n, d) in enumerate(zip(widths, dts))],
        compiler_params=_params(("arbitrary", "arbitrary")),
        name="proj0",
    )(x, g, w, *tabs)


def _proj1_kernel(x_ref, g_ref, w_ref, gcq_ref, wuq_ref, gckv_ref, rc_ref, rs1_ref, rs2_ref,
                  qc_ref, clat_ref, kr_ref, dq_ref, dk_ref, dv_ref, sg_ref):
    u = _rms(x_ref[0], g_ref[...]).astype(BF16)
    c, s1, s2 = rc_ref[...], rs1_ref[...], rs2_ref[...]
    cq = _rms(_dot(u, w_ref[:, 0:768]), gcq_ref[...]).astype(BF16)
    clat_ref[0] = _rms(_dot(u, w_ref[:, 768:1024]), gckv_ref[...])
    dq_ref[0] = (_dot(u, w_ref[:, 1024:1536]) * (D_HD ** -0.5)).astype(BF16)
    dk_ref[0] = _dot(u, w_ref[:, 1536:2048])
    dv_ref[0] = _dot(u, w_ref[:, 2048:2560])
    sg_ref[0] = _silu(_dot(u, w_ref[:, 2560:3584])).astype(BF16)
    kr_ref[0] = _rope(_dot(u, w_ref[:, 3584:3712]), c, s1, s2, C_ROPE // 2)
    qc = _dot(cq, wuq_ref[...])
    for h in range(C_HEADS):
        sl = slice(h * LANES, (h + 1) * LANES)
        qc_ref[0, :, sl] = (_rope(qc[:, sl], c, s1, s2, C_ROPE // 2)
                            * ((C_NOPE + C_ROPE) ** -0.5 * LOG2E)).astype(BF16)


def _proj1(x, g, w, gcq, wuq, gckv, tabs, tm):
    b, t, _ = x.shape
    tok = lambda n: pl.BlockSpec((1, tm, n), lambda ti, bi: (bi, ti, 0))
    const = lambda shape: pl.BlockSpec(shape, lambda ti, bi: (0,) * len(shape))
    tab = pl.BlockSpec((tm, LANES), lambda ti, bi: (ti, 0))
    widths = (1024, 256, 128, 512, 512, 512, 1024)
    dts = (BF16, F32, F32, BF16, F32, F32, BF16)
    return pl.pallas_call(
        _proj1_kernel,
        grid=(t // tm, b),
        in_specs=[tok(D_MODEL), const((1, D_MODEL)), const(w.shape), const((1, C_QL)),
                  const(wuq.shape), const((1, C_KVL)), tab, tab, tab],
        out_specs=[tok(n) for n in widths],
        out_shape=[jax.ShapeDtypeStruct((b, t, n), d) for n, d in zip(widths, dts)],
        compiler_params=_params(("arbitrary", "arbitrary")),
        name="proj1",
    )(x, g, w, gcq, wuq, gckv, *tabs)


def _outproj_kernel(o1_ref, o2_ref, w_ref, g_ref, h_ref, out_ref):
    y = _dot(o1_ref[...], w_ref[0:512, :]) + _dot(o2_ref[...], w_ref[512:1024, :])
    out_ref[...] = h_ref[...] + _rms(y, g_ref[...])


def _outproj(o1, o2, w, g, h, tm):
    n = h.shape[0]
    row = lambda c: pl.BlockSpec((tm, c), lambda i: (i, 0))
    const = lambda shape: pl.BlockSpec(shape, lambda i: (0,) * len(shape))
    return pl.pallas_call(
        _outproj_kernel,
        grid=(n // tm,),
        in_specs=[row(512), row(512), const(w.shape), const((1, D_MODEL)), row(D_MODEL)],
        out_specs=row(D_MODEL),
        out_shape=jax.ShapeDtypeStruct((n, D_MODEL), F32),
        compiler_params=_params(("arbitrary",)),
        name="outproj",
    )(o1, o2, w, g, h)


def _bias_lookup(rb_ref, h, idx):
    def body(r, acc):
        return jnp.where(idx == r, rb_ref[h, r], acc)

    return lax.fori_loop(0, 2 * B_CLIP + 1, body, jnp.zeros(idx.shape, F32)) * LOG2E


def _bias_kernel(rb_ref, out_t_ref, out_s_ref, *, st):
    h = pl.program_id(0)
    far = B_BAND - B_CLIP
    end = rb_ref[h, 2 * B_CLIP] * LOG2E

    def near(shape, key_axis):
        j = lax.broadcasted_iota(jnp.int32, shape, key_axis) + far
        i = lax.broadcasted_iota(jnp.int32, shape, 1 - key_axis)
        return _bias_lookup(rb_ref, h, jnp.clip(B_BAND + i - j, -B_CLIP, B_CLIP) + B_CLIP)

    full_t = jnp.concatenate([jnp.full((far, TQ_B), end, F32), near((WIN_B - far, TQ_B), 0)], axis=0)
    cj = lax.shift_right_logical(lax.broadcasted_iota(jnp.int32, (WIN_B, TQ_B), 0), 6)
    ci = lax.shift_right_logical(lax.broadcasted_iota(jnp.int32, (WIN_B, TQ_B), 1), 6)
    out_t_ref[0] = jnp.where(jnp.logical_and(cj >= ci, cj <= ci + B_LEFT), full_t, NEG)
    out_s_ref[0] = jnp.concatenate([jnp.full((st, far), end, F32), near((st, WIN_B - far), 1)], axis=1)


def _bias_tiles(rel_bias, st):
    return pl.pallas_call(
        functools.partial(_bias_kernel, st=st),
        grid=(B_HEADS,),
        in_specs=[pl.BlockSpec(memory_space=pltpu.SMEM)],
        out_specs=[pl.BlockSpec((1, WIN_B, TQ_B), lambda h: (h // 2, 0, h % 2)),
                   pl.BlockSpec((1, st, WIN_B), lambda h: (h, 0, 0))],
        out_shape=[jax.ShapeDtypeStruct((B_HEADS // 2, WIN_B, 2 * TQ_B), F32),
                   jax.ShapeDtypeStruct((B_HEADS, st, WIN_B), F32)],
        compiler_params=_params(("arbitrary",)),
        name="bias_tile",
    )(rel_bias)


def _ones_rows(n):
    r = lax.broadcasted_iota(jnp.int32, (V_ROWS - LANES, n), 0)
    return jnp.where(r == 0, 1.0, 0.0).astype(BF16)


def _causal_attend_all(n_tiles, n_chains, tq, q_tile, k_rows, vt_cols, finish):
    half = tq // 2
    steps = []
    for qi in range(n_tiles):
        steps += [(qi, j * tq, tq, 0, False) for j in range(qi)]
        steps += [(qi, qi * tq, half, 0, True), (qi, qi * tq + half, half, half, True)]
    qcache = {}

    def scores(step):
        qi, k0, nk, q0, _ = step
        for c in range(n_chains):
            if (c, qi) not in qcache:
                qcache[(c, qi)] = q_tile(c, qi)
        return [_dot_nt(k_rows(c, k0, nk), qcache[(c, qi)][q0:tq, :]) for c in range(n_chains)]

    def chunk_mask(step):
        _, k0, nk, q0, _ = step
        j = lax.broadcasted_iota(jnp.int32, (nk, tq - q0), 0) + (k0 % tq)
        i = lax.broadcasted_iota(jnp.int32, (nk, tq - q0), 1) + q0
        return lax.shift_right_logical(j, 6) <= lax.shift_right_logical(i, 6)

    nxt = scores(steps[0])
    m, acc = [None] * n_chains, [None] * n_chains
    for idx, step in enumerate(steps):
        qi, k0, nk, q0, masked = step
        sts = nxt
        if idx + 1 < len(steps):
            nxt = scores(steps[idx + 1])
        if masked:
            msk = chunk_mask(step)
            sts = [jnp.where(msk, st, NEG) for st in sts]
        first = idx == 0 or steps[idx - 1][0] != qi
        pts, alphas = [], []
        for c in range(n_chains):
            top = jnp.max(sts[c], axis=0, keepdims=True)
            if not first:
                old = m[c][:, q0:tq]
                top = jnp.maximum(old, top)
                alphas.append(jnp.exp2(old - top))
                m[c] = top if q0 == 0 else jnp.concatenate([m[c][:, 0:q0], top], axis=1)
            else:
                m[c] = top
            pts.append(jnp.exp2(sts[c] - top).astype(BF16))
        vt = vt_cols(k0, nk)
        for c in range(n_chains):
            pv = _dot(vt, pts[c])
            if first:
                acc[c] = pv
            elif q0 == 0:
                acc[c] = alphas[c] * acc[c] + pv
            else:
                acc[c] = jnp.concatenate([acc[c][:, 0:q0], alphas[c] * acc[c][:, q0:tq] + pv], axis=1)
        if idx + 1 == len(steps) or steps[idx + 1][0] != qi:
            finish(qi, [a[0:LANES, :] * (1.0 / a[LANES:LANES + 1, :]) for a in acc])


def _lam(lq1, lk1, lq2, lk2, lam_init):
    return (jnp.exp(jnp.sum(lq1[...] * lk1[...], axis=1, keepdims=True))
            - jnp.exp(jnp.sum(lq2[...] * lk2[...], axis=1, keepdims=True)) + lam_init)


def _head_split(q):
    lo = _lane_lo()
    zero = jnp.zeros_like(q)
    return [jnp.where(lo, q, zero), jnp.where(lo, zero, q)]


def _attn_a_kernel(q_ref, k_ref, v_ref, lq1, lk1, lq2, lk2, gsub_ref, sg_ref, o_ref, kb_ref, vt_ref,
                   *, tq, t, lam_init):
    kb_ref[...] = k_ref[0].astype(BF16)
    vt_ref[0:LANES, :] = v_ref[0].T.astype(BF16)
    vt_ref[LANES:V_ROWS, :] = _ones_rows(t)
    lam = _lam(lq1, lk1, lq2, lk2, lam_init)
    tile = lambda i: slice(i * tq, (i + 1) * tq)
    split = {}

    def q_tile(c, qi):
        if qi not in split:
            split[qi] = _head_split(q_ref[0, tile(qi), :])
        return split[qi][c]

    def finish(qi, outs):
        o = outs[0].T - lam * outs[1].T
        o = _rms(o, gsub_ref[...]) * (1.0 - lam_init)
        o_ref[0, tile(qi), :] = (o * sg_ref[0, tile(qi), :].astype(F32)).astype(BF16)

    _causal_attend_all(t // tq, 2, tq, q_tile, lambda c, k0, n: kb_ref[k0:k0 + n, :],
                       lambda k0, n: vt_ref[:, k0:k0 + n], finish)


def _attn_a(q, k, v, lams, gsub, sg, lam_init):
    b, t, _ = q.shape
    tq = min(TQ, t)
    blk = pl.BlockSpec((1, t, LANES), lambda bi, h: (bi, 0, h))
    vec = lambda n: pl.BlockSpec((1, n), lambda bi, h: (0, 0))
    return pl.pallas_call(
        functools.partial(_attn_a_kernel, tq=tq, t=t, lam_init=lam_init),
        grid=(b, A_HEADS),
        in_specs=[blk, blk, blk, vec(A_QK), vec(A_QK), vec(A_QK), vec(A_QK), vec(A_V), blk],
        out_specs=blk,
        out_shape=jax.ShapeDtypeStruct((b, t, A_HEADS * A_V), BF16),
        scratch_shapes=[pltpu.VMEM((t, LANES), BF16), pltpu.VMEM((V_ROWS, t), BF16)],
        compiler_params=_params(("arbitrary", "arbitrary")),
        name="attn_a",
    )(q, k, v, *lams, gsub, sg)


def _attn_b_kernel(q_ref, k_ref, v_ref, bias_ref, sg_ref, o_ref, kb_ref, vt_ref, *, t):
    kb_ref[0:B_BAND, :] = jnp.zeros((B_BAND, LANES), BF16)
    kb_ref[B_BAND:B_BAND + t, :] = k_ref[0].astype(BF16)
    vt_ref[:, 0:B_BAND] = jnp.zeros((V_ROWS, B_BAND), BF16)
    vt_ref[0:LANES, B_BAND:B_BAND + t] = v_ref[0].T.astype(BF16)
    vt_ref[LANES:V_ROWS, B_BAND:B_BAND + t] = _ones_rows(t)
    key = lax.broadcasted_iota(jnp.int32, (WIN_B, 1), 0)
    row_lo = _row_lo()

    def tiles(q0s, first_valids):
        def scores(n):
            qa, qb = _head_split(q_ref[0, pl.ds(q0s[n], TQ_B), :])
            st = _dot_nt(kb_ref[pl.ds(q0s[n], WIN_B), :], jnp.concatenate([qa, qb], axis=0)) + bias_ref[0]
            if first_valids[n] > 0:
                st = jnp.where(key < first_valids[n], NEG, st)
            return st

        nxt = scores(0)
        for n in range(len(q0s)):
            st = nxt
            if n + 1 < len(q0s):
                nxt = scores(n + 1)
            pt = jnp.exp2(st - jnp.max(st, axis=0, keepdims=True)).astype(BF16)
            acc = _dot(vt_ref[:, pl.ds(q0s[n], WIN_B)], pt)
            inv = 1.0 / acc[LANES:LANES + 1, :]
            ot = jnp.where(row_lo, acc[0:LANES, 0:TQ_B] * inv[:, 0:TQ_B],
                           acc[0:LANES, TQ_B:2 * TQ_B] * inv[:, TQ_B:2 * TQ_B])
            rows = pl.ds(q0s[n], TQ_B)
            o_ref[0, rows, :] = (ot.T * sg_ref[0, rows, :].astype(F32)).astype(BF16)

    n_edge = min(B_BAND, t) // TQ_B
    tiles([qi * TQ_B for qi in range(n_edge)], [B_BAND - qi * TQ_B for qi in range(n_edge)])
    per_iter = 4

    def body(i, carry):
        q0 = pl.multiple_of(n_edge * TQ_B + i * per_iter * TQ_B, per_iter * TQ_B)
        tiles([q0 + n * TQ_B for n in range(per_iter)], [0] * per_iter)
        return carry

    lax.fori_loop(0, (t - n_edge * TQ_B) // (per_iter * TQ_B), body, 0)


def _attn_b(q, k, v, bias_t, sg):
    b, t, _ = q.shape
    assert t % (4 * TQ_B) == 0
    blk = pl.BlockSpec((1, t, LANES), lambda bi, h: (bi, 0, h))
    sgblk = pl.BlockSpec((1, t, LANES), lambda bi, h: (bi, 0, h + A_HEADS))
    bblk = pl.BlockSpec((1, WIN_B, 2 * TQ_B), lambda bi, h: (h, 0, 0))
    return pl.pallas_call(
        functools.partial(_attn_b_kernel, t=t),
        grid=(b, B_HEADS // 2),
        in_specs=[blk, blk, blk, bblk, sgblk],
        out_specs=blk,
        out_shape=jax.ShapeDtypeStruct((b, t, B_HEADS * B_HD), BF16),
        scratch_shapes=[pltpu.VMEM((B_BAND + t, LANES), BF16), pltpu.VMEM((V_ROWS, B_BAND + t), BF16)],
        compiler_params=_params(("arbitrary", "arbitrary")),
        name="attn_b",
    )(q, k, v, bias_t, sg)


def _attn_c_kernel(q_ref, clat_ref, kr_ref, wuk_ref, wuv_ref, sg_ref, o_ref, kc_ref, vt_ref, *, tq, t):
    cl = clat_ref[0].astype(BF16)
    kr = kr_ref[0]
    for hh in range(2):
        kc_ref[hh] = (_dot(cl, wuk_ref[hh]) + kr).astype(BF16)
    vt_ref[0:LANES, :] = _dot(cl, wuv_ref[0]).T.astype(BF16)
    vt_ref[LANES:V_ROWS, :] = _ones_rows(t)
    tile = lambda i: slice(i * tq, (i + 1) * tq)
    row_lo = _row_lo()

    def finish(qi, outs):
        o = jnp.where(row_lo, outs[0], outs[1]).T
        o_ref[0, tile(qi), :] = (o * sg_ref[0, tile(qi), :].astype(F32)).astype(BF16)

    _causal_attend_all(t // tq, 2, tq, lambda c, qi: q_ref[0, tile(qi), c * LANES:(c + 1) * LANES],
                       lambda c, k0, n: kc_ref[c, k0:k0 + n, :], lambda k0, n: vt_ref[:, k0:k0 + n], finish)


def _attn_c(qc, clat, kr, wuk, wuv, sg):
    b, t, _ = qc.shape
    tq = min(TQ, t)
    return pl.pallas_call(
        functools.partial(_attn_c_kernel, tq=tq, t=t),
        grid=(b, C_HEADS // 2),
        in_specs=[pl.BlockSpec((1, t, 2 * LANES), lambda bi, h: (bi, 0, h)),
                  pl.BlockSpec((1, t, C_KVL), lambda bi, h: (bi, 0, 0)),
                  pl.BlockSpec((1, t, LANES), lambda bi, h: (bi, 0, 0)),
                  pl.BlockSpec((2, C_KVL, LANES), lambda bi, h: (h, 0, 0)),
                  pl.BlockSpec((1, C_KVL, LANES), lambda bi, h: (h, 0, 0)),
                  pl.BlockSpec((1, t, LANES), lambda bi, h: (bi, 0, h))],
        out_specs=pl.BlockSpec((1, t, LANES), lambda bi, h: (bi, 0, h)),
        out_shape=jax.ShapeDtypeStruct((b, t, C_HEADS * C_V), BF16),
        scratch_shapes=[pltpu.VMEM((2, t, LANES), BF16), pltpu.VMEM((V_ROWS, t), BF16)],
        compiler_params=_params(("arbitrary", "arbitrary")),
        name="attn_c",
    )(qc, clat, kr, wuk, wuv, sg)


def _suffix_sums_t(msp, negut, tk):
    mb = msp.astype(BF16)
    if msp.shape[0] == tk:
        return _dot(negut, mb)
    late = _dot(negut, mb[tk:2 * tk, :])
    return jnp.concatenate([_dot(negut, mb[0:tk, :]) + late[0:1, :], late], axis=0)


def _stick_units_t(qs, ks, vts, negut, runs, tri_t, tk):
    n = len(qs)

    def diag_only(x):
        return x if tri_t is None else jnp.concatenate([x[0:tk, :], jnp.where(tri_t, x[tk:2 * tk, :], 0.0)], axis=0)

    zts, stage1, out = [None] * n, [None] * n, [None] * n
    for step in range(n + 2):
        if step < n:
            zts[step] = _dot_nt(ks[step], qs[step])
        c = step - 1
        if 0 <= c < n:
            later = _suffix_sums_t(diag_only(_softplus(zts[c])), negut, tk)
            stage1[c] = (zts[c] + later, later[0:1, :])
        c = step - 2
        if 0 <= c < n:
            x, total = stage1[c]
            if runs is not None:
                x = x + runs[c]
            out[c] = (total, _dot(vts[c], diag_only(jnp.exp(x)).astype(BF16)))
    return out


def _attn_d_kernel(q_ref, k_ref, v_ref, negut_ref, sg_ref, o_ref, kb_ref, vt_ref, *, tk, t, nq):
    qt = pl.program_id(2)

    @pl.when(qt == 0)
    def _():
        kb_ref[0:tk, :] = jnp.zeros((tk, LANES), BF16)
        vt_ref[:, 0:tk] = jnp.zeros((LANES, tk), BF16)
        kb_ref[tk:tk + t, :] = k_ref[0].astype(BF16)
        vt_ref[:, tk:tk + t] = v_ref[0].T.astype(BF16)

    qs, tile_of = [], []
    for h in range(nq):
        qs += _head_split(q_ref[0, h * tk:(h + 1) * tk, :])
        tile_of += [h, h]
    j = lax.broadcasted_iota(jnp.int32, (tk, tk), 0)
    i = lax.broadcasted_iota(jnp.int32, (tk, tk), 1)
    tri_t = j < i
    wins = [pl.ds(pl.multiple_of((nq * qt + h) * tk, tk), 2 * tk) for h in tile_of]
    first = _stick_units_t(qs, [kb_ref[w, :] for w in wins], [vt_ref[:, w] for w in wins], negut_ref[...],
                           None, tri_t, tk)
    runs = [f[0] for f in first]
    accs = [f[1] for f in first]

    def top(rs):
        return functools.reduce(jnp.maximum, [jnp.max(r) for r in rs])

    def cond(carry):
        return jnp.logical_and(carry[0] < nq * qt + nq - 2, carry[1] > -104.0)

    def body(carry):
        n, _, runs, accs = carry
        tiles = [pl.ds(pl.multiple_of(jnp.maximum(nq * qt + h - 1 - n, 0) * tk, tk), tk) for h in tile_of]
        res = _stick_units_t(qs, [kb_ref[w, :] for w in tiles], [vt_ref[:, w] for w in tiles],
                             negut_ref[...], runs, None, tk)
        runs = [r + d for r, (d, _) in zip(runs, res)]
        accs = [a + c for a, (_, c) in zip(accs, res)]
        return n + 1, top(runs), runs, accs

    _, _, _, accs = lax.while_loop(cond, body, (jnp.int32(0), top(runs), runs, accs))
    row_lo = _row_lo()
    for h in range(nq):
        ot = jnp.where(row_lo, accs[2 * h], accs[2 * h + 1])
        rows = slice(h * tk, (h + 1) * tk)
        o_ref[0, rows, :] = (ot.T * sg_ref[0, rows, :].astype(F32)).astype(BF16)


def _attn_d(q, k, v, negut, sg):
    b, t, _ = q.shape
    tk = min(TQ_D, t // NQ_D)
    qblk = pl.BlockSpec((1, NQ_D * tk, LANES), lambda bi, h, qi: (bi, qi, h))
    sgblk = pl.BlockSpec((1, NQ_D * tk, LANES), lambda bi, h, qi: (bi, qi, h + C_HEADS // 2))
    kvblk = pl.BlockSpec((1, t, LANES), lambda bi, h, qi: (bi, 0, h))
    ublk = pl.BlockSpec((tk, tk), lambda bi, h, qi: (0, 0))
    return pl.pallas_call(
        functools.partial(_attn_d_kernel, tk=tk, t=t, nq=NQ_D),
        grid=(b, D_HEADS // 2, t // (NQ_D * tk)),
        in_specs=[qblk, kvblk, kvblk, ublk, sgblk],
        out_specs=qblk,
        out_shape=jax.ShapeDtypeStruct((b, t, D_HEADS * D_HD), BF16),
        scratch_shapes=[pltpu.VMEM((tk + t, LANES), BF16), pltpu.VMEM((LANES, tk + t), BF16)],
        compiler_params=_params(("arbitrary", "arbitrary", "arbitrary")),
        name="attn_d",
    )(q, k, v, negut, sg)


def _two_part_softmax(q, kct, kn, vc, vn, bc=None, bn=None, vc_t=False):
    sc = _dot(q, kct)
    sn = _dot_nt(q, kn)
    if bc is not None:
        sc = sc + bc
        sn = sn + bn
    m = jnp.maximum(jnp.max(sc, axis=1, keepdims=True), jnp.max(sn, axis=1, keepdims=True))
    pc = jnp.exp2(sc - m)
    pn = jnp.exp2(sn - m)
    l = jnp.sum(pc, axis=1, keepdims=True) + jnp.sum(pn, axis=1, keepdims=True)
    pvc = _dot_nt(pc.astype(BF16), vc) if vc_t else _dot(pc.astype(BF16), vc)
    return (pvc + _dot(pn.astype(BF16), vn)) * (1.0 / l)


def _samp_a_kernel(q_ref, ck_ref, cv_hbm, nk_ref, nv_ref, lq1, lk1, lq2, lk2, gsub_ref, sg_ref, o_ref,
                   vbuf_ref, sem_ref, *, lam_init):
    b = pl.program_id(0)
    copies = [pltpu.make_async_copy(cv_hbm.at[b, :, h, :], vbuf_ref.at[h], sem_ref.at[h]) for h in range(A_HEADS)]
    for cp in copies:
        cp.start()
    lam = _lam(lq1, lk1, lq2, lk2, lam_init)
    for h in range(A_HEADS):
        sl = slice(h * LANES, (h + 1) * LANES)
        copies[h].wait()
        kct, vc = ck_ref[0, h].astype(BF16), vbuf_ref[h].astype(BF16)
        kn, vn = nk_ref[0, :, sl].astype(BF16), nv_ref[0, :, sl].astype(BF16)
        q0, q1 = _head_split(q_ref[0, :, sl])
        o0 = _two_part_softmax(q0, kct, kn, vc, vn)
        o1 = _two_part_softmax(q1, kct, kn, vc, vn)
        o = _rms(o0 - lam * o1, gsub_ref[...]) * (1.0 - lam_init)
        o_ref[0, :, sl] = (o * sg_ref[0, :, sl].astype(F32)).astype(BF16)


def _samp_b_kernel(q_ref, ck_ref, cv_ref, nk_ref, nv_ref, bias_ref, sg_ref, o_ref, *, band, st):
    lo = _lane_lo()
    for hp in range(B_HEADS // 2):
        sl = slice(hp * LANES, (hp + 1) * LANES)
        kct, vct = ck_ref[0, hp].astype(BF16), cv_ref[0, hp].astype(BF16)
        kn, vn = nk_ref[0, :, sl].astype(BF16), nv_ref[0, :, sl].astype(BF16)
        outs = []
        for hh, qm in enumerate(_head_split(q_ref[0, :, sl])):
            bias = bias_ref[2 * hp + hh]
            outs.append(_two_part_softmax(qm, kct, kn, vct, vn, bias[:, 0:band], bias[:, band:band + st],
                                          vc_t=True))
        o = jnp.where(lo, outs[0], outs[1])
        o_ref[0, :, sl] = (o * sg_ref[0, :, sl].astype(F32)).astype(BF16)


def _samp_c_kernel(q_ref, clat_ref, krp_ref, nclat_ref, nkr_ref, wukt_ref, wuv_ref, sg_ref, o_ref, *, st):
    lo = _lane_lo()
    cl = clat_ref[0].astype(BF16)
    ncl = nclat_ref[0].astype(BF16)
    past = krp_ref.shape[2]
    krct = jnp.concatenate([jnp.zeros((C_NOPE, past), BF16), krp_ref[0].astype(BF16),
                            jnp.zeros((LANES - C_NOPE - C_ROPE, past), BF16)], axis=0)
    krn = nkr_ref[0].astype(BF16)
    qh = [q_ref[0, :, h * LANES:(h + 1) * LANES] for h in range(C_HEADS)]
    q_all = jnp.concatenate(qh, axis=0)
    q_lat = jnp.concatenate([_dot(qh[h], wukt_ref[h]) for h in range(C_HEADS)], axis=0).astype(BF16)
    sc = _dot_nt(q_lat, cl) + _dot(q_all, krct)
    sn = _dot_nt(q_lat, ncl) + _dot_nt(q_all, krn)
    m = jnp.maximum(jnp.max(sc, axis=1, keepdims=True), jnp.max(sn, axis=1, keepdims=True))
    pc = jnp.exp2(sc - m)
    pn = jnp.exp2(sn - m)
    l = jnp.sum(pc, axis=1, keepdims=True) + jnp.sum(pn, axis=1, keepdims=True)
    o_lat = ((_dot(pc.astype(BF16), cl) + _dot(pn.astype(BF16), ncl)) * (1.0 / l)).astype(BF16)
    for hp in range(C_HEADS // 2):
        sl = slice(hp * LANES, (hp + 1) * LANES)
        oa = _dot(o_lat[(2 * hp) * st:(2 * hp + 1) * st, :], wuv_ref[hp])
        ob = _dot(o_lat[(2 * hp + 1) * st:(2 * hp + 2) * st, :], wuv_ref[hp])
        o_ref[0, :, sl] = (jnp.where(lo, oa, ob) * sg_ref[0, :, sl].astype(F32)).astype(BF16)


def _stick_unit(q, k, v, negu, run, tri, kv_t=False):
    z = _dot(q, k) if kv_t else _dot_nt(q, k)
    sp = _softplus(z)
    msp = sp if tri is None else jnp.where(tri, sp, 0.0)
    later = _dot(msp.astype(BF16), negu)
    x = z - sp + later
    if run is not None:
        x = x + run
    a = jnp.exp(x)
    if tri is not None:
        a = jnp.where(tri, a, 0.0)
    total = later[:, 0:1] - msp[:, 0:1]
    return total, (_dot_nt(a.astype(BF16), v) if kv_t else _dot(a.astype(BF16), v))


def _neg_suffix(n):
    r = lax.broadcasted_iota(jnp.int32, (n, n), 0)
    c = lax.broadcasted_iota(jnp.int32, (n, n), 1)
    return jnp.where(r > c, -1.0, 0.0).astype(BF16)


def _samp_d_kernel(q_ref, ck_ref, cv_ref, nk_ref, nv_ref, sg_ref, o_ref, *, past, st, tk):
    lo = _lane_lo()
    negu_new = _neg_suffix(st)
    r = lax.broadcasted_iota(jnp.int32, (st, st), 0)
    c = lax.broadcasted_iota(jnp.int32, (st, st), 1)
    tri = c < r
    negu = _neg_suffix(tk)
    n_chunks = past // tk
    for hp in range(D_HEADS // 2):
        sl = slice(hp * LANES, (hp + 1) * LANES)
        kn, vn = nk_ref[0, :, sl].astype(BF16), nv_ref[0, :, sl].astype(BF16)
        qa, qb = _head_split(q_ref[0, :, sl])
        run_a, acc_a = _stick_unit(qa, kn, vn, negu_new, None, tri)
        run_b, acc_b = _stick_unit(qb, kn, vn, negu_new, None, tri)

        def cond(carry):
            return jnp.logical_and(carry[0] < n_chunks, carry[1] > -104.0)

        def body(carry, qa=qa, qb=qb, hp=hp):
            n, _, run_a, acc_a, run_b, acc_b = carry
            keys = pl.ds(pl.multiple_of((n_chunks - 1 - n) * tk, tk), tk)
            kct, vct = ck_ref[0, hp, :, keys].astype(BF16), cv_ref[0, hp, :, keys].astype(BF16)
            da, ca = _stick_unit(qa, kct, vct, negu, run_a, None, kv_t=True)
            db, cb = _stick_unit(qb, kct, vct, negu, run_b, None, kv_t=True)
            run_a, run_b = run_a + da, run_b + db
            return n + 1, jnp.maximum(jnp.max(run_a), jnp.max(run_b)), run_a, acc_a + ca, run_b, acc_b + cb

        init = (jnp.int32(0), jnp.maximum(jnp.max(run_a), jnp.max(run_b)), run_a, acc_a, run_b, acc_b)
        _, _, _, acc_a, _, acc_b = lax.while_loop(cond, body, init)
        o = jnp.where(lo, acc_a, acc_b)
        o_ref[0, :, sl] = (o * sg_ref[0, :, sl].astype(F32)).astype(BF16)


def _samp_call(body, name, b, st, ins, specs, scratch=()):
    return pl.pallas_call(
        body,
        grid=(b,),
        in_specs=specs,
        out_specs=pl.BlockSpec((1, st, 512), lambda bi: (bi, 0, 0)),
        out_shape=jax.ShapeDtypeStruct((b, st, 512), BF16),
        scratch_shapes=list(scratch),
        compiler_params=_params(("arbitrary",)),
        name=name,
    )(*ins)


def _per_batch(shape, col=0):
    return pl.BlockSpec((1,) + tuple(shape[1:]), lambda bi: (bi,) + (0,) * (len(shape) - 2) + (col,))


def _whole(shape):
    return pl.BlockSpec(tuple(shape), lambda bi: (0,) * len(shape))


def _rope_tables(pos, rot, period, offset):
    half = rot // 2
    inv_freq = ROPE_THETA ** (-jnp.arange(half, dtype=F32) / half)
    ang = pos.astype(F32)[:, None] * inv_freq[None, :]
    cos, sin = jnp.cos(ang), jnp.sin(ang)
    n = pos.shape[0]
    c = jnp.ones((n, period), F32).at[:, offset:offset + half].set(cos).at[:, offset + half:offset + rot].set(cos)
    s1 = jnp.zeros((n, period), F32).at[:, offset:offset + half].set(-sin)
    s2 = jnp.zeros((n, period), F32).at[:, offset + half:offset + rot].set(sin)
    reps = LANES // period
    return tuple(jnp.tile(x, (1, reps)) for x in (c, s1, s2))


def kernel(x_prompt, x_sample, cache_a_k, cache_a_v, cache_b_k, cache_b_v, cache_c_latent, cache_c_krope,
           cache_d_k, cache_d_v, g_pre0, w_in0, lam_q1, lam_k1, lam_q2, lam_k2, g_sub_a, rel_bias_b, w_out0,
           g_post0, g_pre1, w_in1, g_cq, w_uq, g_ckv, w_uk, w_uv, w_out1, g_post1):
    b, t, _ = x_prompt.shape
    sb, st, _ = x_sample.shape
    past = cache_a_k.shape[1]
    band = cache_b_k.shape[1]
    assert band == B_BAND and t % TQ == 0 and past % TQ_D == 0
    ns = sb * st
    row = lambda x: x.reshape(1, -1)

    w0 = w_in0.astype(BF16)
    wo0 = w_out0.astype(BF16)
    wo1 = w_out1.astype(BF16)
    w1 = jnp.concatenate([w_in1[:, 0:1024], w_in1[:, 1056:3616], jnp.zeros((D_MODEL, 64), F32),
                          w_in1[:, 1024:1056], jnp.zeros((D_MODEL, 32), F32)], axis=1).astype(BF16)
    wuq = jnp.pad(w_uq.reshape(C_QL, C_HEADS, C_NOPE + C_ROPE), ((0, 0), (0, 0), (0, 32)))
    wuq = wuq.reshape(C_QL, C_HEADS * LANES).astype(BF16)
    wuk = jnp.pad(w_uk.transpose(1, 0, 2), ((0, 0), (0, 0), (0, LANES - C_NOPE))).astype(BF16)
    wuv = w_uv.reshape(C_KVL, C_HEADS // 2, 2 * C_V).transpose(1, 0, 2).astype(BF16)
    n_u = min(TQ_D, t // NQ_D)
    negut = -jnp.triu(jnp.ones((n_u, n_u), BF16))
    lams = tuple(row(x) for x in (lam_q1, lam_k1, lam_q2, lam_k2))
    lam_init = 0.8 - 0.6 * math.exp(-0.3 * 0)

    pos_p = jnp.arange(t)
    pos_s = past + jnp.arange(st)
    tabs0_p = _rope_tables(pos_p, A_ROT, A_QK, 0)
    tabs0_s = tuple(jnp.tile(x, (sb, 1)) for x in _rope_tables(pos_s, A_ROT, A_QK, 0))
    tabs1_p = _rope_tables(pos_p, C_ROPE, LANES, C_NOPE)
    tabs1_s = tuple(jnp.tile(x, (sb, 1)) for x in _rope_tables(pos_s, C_ROPE, LANES, C_NOPE))

    bias_t, bias_s = _bias_tiles(rel_bias_b, st)

    aq, ak, av, bq, bk, bv, sg0, avb = _proj0(x_prompt, row(g_pre0), w0, tabs0_p, 512)
    oa = _attn_a(aq, ak, avb, lams, row(g_sub_a), sg0, lam_init)
    ob = _attn_b(bq, bk, bv, bias_t, sg0)
    h1 = _outproj(oa.reshape(b * t, 512), ob.reshape(b * t, 512), wo0, row(g_post0),
                  x_prompt.reshape(b * t, D_MODEL), 1024)

    def keys_minor(x):
        n = x.shape[1]
        return jnp.moveaxis(x, 1, -1).reshape(sb, -1, LANES, n)

    xs = x_sample.reshape(1, ns, D_MODEL)
    s_out = _proj0(xs, row(g_pre0), w0, tabs0_s, ns)
    aq_s, ak_s, av_s, bq_s, bk_s, bv_s, sg0_s, _ = (x.reshape(sb, st, -1) for x in s_out)
    vec = lambda n: _whole((1, n))
    oa_s = _samp_call(
        functools.partial(_samp_a_kernel, lam_init=lam_init), "samp_a", sb, st,
        (aq_s, keys_minor(cache_a_k), cache_a_v, ak_s, av_s,
         *lams, row(g_sub_a), sg0_s),
        [_per_batch((sb, st, 512)), _per_batch((sb, A_HEADS, LANES, past)), pl.BlockSpec(memory_space=pl.ANY),
         _per_batch((sb, st, 512)), _per_batch((sb, st, 512)), vec(A_QK), vec(A_QK), vec(A_QK), vec(A_QK),
         vec(A_V), _per_batch((sb, st, 512))],
        scratch=[pltpu.VMEM((A_HEADS, past, A_V), F32), pltpu.SemaphoreType.DMA((A_HEADS,))])
    ob_s = _samp_call(
        functools.partial(_samp_b_kernel, band=band, st=st), "samp_b", sb, st,
        (bq_s, keys_minor(cache_b_k), keys_minor(cache_b_v), bk_s, bv_s, bias_s, sg0_s),
        [_per_batch((sb, st, 512)), _per_batch((sb, B_HEADS // 2, LANES, band)),
         _per_batch((sb, B_HEADS // 2, LANES, band)),
         _per_batch((sb, st, 512)), _per_batch((sb, st, 512)),
         pl.BlockSpec((B_HEADS, st, WIN_B), lambda bi: (0, 0, 0)), _per_batch((sb, st, 512), col=1)])
    hs1 = _outproj(oa_s.reshape(ns, 512), ob_s.reshape(ns, 512), wo0, row(g_post0),
                   x_sample.reshape(ns, D_MODEL), ns)

    qc, clat, kr, dq, dk, dv, sg1 = _proj1(h1.reshape(b, t, D_MODEL), row(g_pre1), w1, row(g_cq), wuq,
                                           row(g_ckv), tabs1_p, 512)
    oc = _attn_c(qc, clat, kr, wuk, wuv, sg1)
    od = _attn_d(dq, dk, dv, negut, sg1)
    h2 = _outproj(oc.reshape(b * t, 512), od.reshape(b * t, 512), wo1, row(g_post1), h1, 1024)

    s_out = _proj1(hs1.reshape(1, ns, D_MODEL), row(g_pre1), w1, row(g_cq), wuq, row(g_ckv), tabs1_s, ns)
    qc_s, clat_s, kr_s, dq_s, dk_s, dv_s, sg1_s = (x.reshape(sb, st, -1) for x in s_out)
    krt = jnp.moveaxis(cache_c_krope, 1, -1)
    wukt = wuk.transpose(0, 2, 1)
    oc_s = _samp_call(
        functools.partial(_samp_c_kernel, st=st), "samp_c", sb, st,
        (qc_s, cache_c_latent, krt, clat_s, kr_s, wukt, wuv, sg1_s),
        [_per_batch((sb, st, 1024)), _per_batch((sb, past, C_KVL)), _per_batch((sb, C_ROPE, past)),
         _per_batch((sb, st, C_KVL)), _per_batch((sb, st, LANES)), _whole(wukt.shape), _whole(wuv.shape),
         _per_batch((sb, st, 512))])
    od_s = _samp_call(
        functools.partial(_samp_d_kernel, past=past, st=st, tk=TQ_D), "samp_d", sb, st,
        (dq_s, keys_minor(cache_d_k), keys_minor(cache_d_v), dk_s, dv_s, sg1_s),
        [_per_batch((sb, st, 512)), _per_batch((sb, D_HEADS // 2, LANES, past)),
         _per_batch((sb, D_HEADS // 2, LANES, past)),
         _per_batch((sb, st, 512)), _per_batch((sb, st, 512)), _per_batch((sb, st, 512), col=1)])
    hs2 = _outproj(oc_s.reshape(ns, 512), od_s.reshape(ns, 512), wo1, row(g_post1), hs1, ns)

    b_rows = min(B_BAND, t)
    return (h2.reshape(b, t, D_MODEL), hs2.reshape(sb, st, D_MODEL),
            ak.reshape(b, t, A_HEADS, 2, A_QK), av,
            bk[:, t - b_rows:].reshape(b, b_rows, B_HEADS, B_HD), bv[:, t - b_rows:].reshape(b, b_rows, B_HEADS, B_HD),
            clat, kr[:, :, C_NOPE:C_NOPE + C_ROPE],
            dk.reshape(b, t, D_HEADS, D_HD), dv.reshape(b, t, D_HEADS, D_HD),
            ak_s.reshape(sb, st, A_HEADS, 2, A_QK), av_s.reshape(sb, st, A_HEADS, A_V),
            bk_s.reshape(sb, st, B_HEADS, B_HD), bv_s.reshape(sb, st, B_HEADS, B_HD),
            clat_s, kr_s[:, :, C_NOPE:C_NOPE + C_ROPE],
            dk_s.reshape(sb, st, D_HEADS, D_HD), dv_s.reshape(sb, st, D_HEADS, D_HD))
```

```python
import functools
import math

import jax
import jax.numpy as jnp
from jax import lax
from jax.experimental import pallas as pl
from jax.experimental.pallas import tpu as pltpu

F32 = jnp.float32
BF16 = jnp.bfloat16

D_MODEL = 1024
CHUNK = 64
ROPE_THETA = 500000.0
NORM_EPS = 1e-6
A_HEADS, A_QK, A_V, A_ROT = 4, 64, 128, 16
B_HEADS, B_HD, B_LEFT, B_CLIP = 8, 64, 8, 128
C_HEADS, C_NOPE, C_ROPE, C_V, C_QL, C_KVL = 8, 64, 32, 64, 768, 256
D_HEADS, D_HD = 8, 64
B_BAND = B_LEFT * CHUNK
LANES = 128
NEG = -1e30
VMEM_LIMIT = 56 * 1024 * 1024

TQ = 512
TQ_D = 256
NQ_D = 8
TQ_B = 128
WIN_B = B_BAND + TQ_B
LOG2E = 1.4426950408889634
V_ROWS = 144


def _dot(a, b):
    return jnp.dot(a, b, preferred_element_type=F32)


def _dot_nt(a, b):
    return lax.dot_general(a, b, (((1,), (1,)), ((), ())), preferred_element_type=F32)


def _rms(x, g):
    ms = jnp.mean(x * x, axis=-1, keepdims=True)
    return x * lax.rsqrt(ms + NORM_EPS) * g


def _silu(x):
    return x * (1.0 / (1.0 + jnp.exp(-x)))


def _softplus(z):
    return jnp.maximum(z, 0.0) + jnp.log(1.0 + jnp.exp(-jnp.abs(z)))


def _rope(z, c, s1, s2, shift):
    return z * c + pltpu.roll(z, LANES - shift, 1) * s1 + pltpu.roll(z, shift, 1) * s2


def _lane_lo():
    return lax.broadcasted_iota(jnp.int32, (1, LANES), 1) < 64


def _row_lo():
    return lax.broadcasted_iota(jnp.int32, (LANES, 1), 0) < 64


def _params(sem):
    return pltpu.CompilerParams(dimension_semantics=sem, vmem_limit_bytes=VMEM_LIMIT)


def _proj0_kernel(x_ref, g_ref, w_ref, rc_ref, rs1_ref, rs2_ref,
                  aq_ref, ak_ref, av_ref, bq_ref, bk_ref, bv_ref, sg_ref, avb_ref):
    u = _rms(x_ref[0], g_ref[...]).astype(BF16)
    c, s1, s2 = rc_ref[...], rs1_ref[...], rs2_ref[...]
    zq = _dot(u, w_ref[:, 0:512])
    zk = _dot(u, w_ref[:, 512:1024])
    for v in range(4):
        sl = slice(v * LANES, (v + 1) * LANES)
        aq_ref[0, :, sl] = (_rope(zq[:, sl], c, s1, s2, A_ROT // 2) * (A_QK ** -0.5 * LOG2E)).astype(BF16)
        ak_ref[0, :, sl] = _rope(zk[:, sl], c, s1, s2, A_ROT // 2)
    zv = _dot(u, w_ref[:, 1024:1536])
    avb_ref[0] = zv.astype(BF16)
    for h in range(A_HEADS):
        av_ref[0, :, h, :] = zv[:, h * A_V:(h + 1) * A_V]
    bq_ref[0] = (_dot(u, w_ref[:, 1536:2048]) * (B_HD ** -0.5 * LOG2E)).astype(BF16)
    bk_ref[0] = _dot(u, w_ref[:, 2048:2560])
    bv_ref[0] = _dot(u, w_ref[:, 2560:3072])
    sg_ref[0] = _silu(_dot(u, w_ref[:, 3072:4096])).astype(BF16)


def _proj0(x, g, w, tabs, tm):
    b, t, _ = x.shape
    tok = lambda n: pl.BlockSpec((1, tm, n), lambda ti, bi: (bi, ti, 0))
    const = lambda shape: pl.BlockSpec(shape, lambda ti, bi: (0,) * len(shape))
    tab = pl.BlockSpec((tm, LANES), lambda ti, bi: (ti, 0))
    widths = (512, 512, 512, 512, 512, 512, 1024, 512)
    dts = (BF16, F32, F32, BF16, F32, F32, BF16, BF16)
    av4 = lambda i: i == 2
    return pl.pallas_call(
        _proj0_kernel,
        grid=(t // tm, b),
        in_specs=[tok(D_MODEL), const((1, D_MODEL)), const(w.shape), tab, tab, tab],
        out_specs=[pl.BlockSpec((1, tm, A_HEADS, A_V), lambda ti, bi: (bi, ti, 0, 0)) if av4(i) else tok(n)
                   for i, n in enumerate(widths)],
        out_shape=[jax.ShapeDtypeStruct((b, t, A_HEADS, A_V) if av4(i) else (b, t, n), d)
                   for i, (n, d) in enumerate(zip(widths, dts))],
        compiler_params=_params(("arbitrary", "arbitrary")),
        name="proj0",
    )(x, g, w, *tabs)


def _proj1_kernel(x_ref, g_ref, w_ref, gcq_ref, wuq_ref, gckv_ref, rc_ref, rs1_ref, rs2_ref,
                  qc_ref, clat_ref, kr_ref, dq_ref, dk_ref, dv_ref, sg_ref):
    u = _rms(x_ref[0], g_ref[...]).astype(BF16)
    c, s1, s2 = rc_ref[...], rs1_ref[...], rs2_ref[...]
    cq = _rms(_dot(u, w_ref[:, 0:768]), gcq_ref[...]).astype(BF16)
    clat_ref[0] = _rms(_dot(u, w_ref[:, 768:1024]), gckv_ref[...])
    dq_ref[0] = (_dot(u, w_ref[:, 1024:1536]) * (D_HD ** -0.5)).astype(BF16)
    dk_ref[0] = _dot(u, w_ref[:, 1536:2048])
    dv_ref[0] = _dot(u, w_ref[:, 2048:2560])
    sg_ref[0] = _silu(_dot(u, w_ref[:, 2560:3584])).astype(BF16)
    kr_ref[0] = _rope(_dot(u, w_ref[:, 3584:3712]), c, s1, s2, C_ROPE // 2)
    qc = _dot(cq, wuq_ref[...])
    for h in range(C_HEADS):
        sl = slice(h * LANES, (h + 1) * LANES)
        qc_ref[0, :, sl] = (_rope(qc[:, sl], c, s1, s2, C_ROPE // 2)
                            * ((C_NOPE + C_ROPE) ** -0.5 * LOG2E)).astype(BF16)


def _proj1(x, g, w, gcq, wuq, gckv, tabs, tm):
    b, t, _ = x.shape
    tok = lambda n: pl.BlockSpec((1, tm, n), lambda ti, bi: (bi, ti, 0))
    const = lambda shape: pl.BlockSpec(shape, lambda ti, bi: (0,) * len(shape))
    tab = pl.BlockSpec((tm, LANES), lambda ti, bi: (ti, 0))
    widths = (1024, 256, 128, 512, 512, 512, 1024)
    dts = (BF16, F32, F32, BF16, F32, F32, BF16)
    return pl.pallas_call(
        _proj1_kernel,
        grid=(t // tm, b),
        in_specs=[tok(D_MODEL), const((1, D_MODEL)), const(w.shape), const((1, C_QL)),
                  const(wuq.shape), const((1, C_KVL)), tab, tab, tab],
        out_specs=[tok(n) for n in widths],
        out_shape=[jax.ShapeDtypeStruct((b, t, n), d) for n, d in zip(widths, dts)],
        compiler_params=_params(("arbitrary", "arbitrary")),
        name="proj1",
    )(x, g, w, gcq, wuq, gckv, *tabs)


def _outproj_kernel(o1_ref, o2_ref, w_ref, g_ref, h_ref, out_ref):
    y = _dot(o1_ref[...], w_ref[0:512, :]) + _dot(o2_ref[...], w_ref[512:1024, :])
    out_ref[...] = h_ref[...] + _rms(y, g_ref[...])


def _outproj(o1, o2, w, g, h, tm):
    n = h.shape[0]
    row = lambda c: pl.BlockSpec((tm, c), lambda i: (i, 0))
    const = lambda shape: pl.BlockSpec(shape, lambda i: (0,) * len(shape))
    return pl.pallas_call(
        _outproj_kernel,
        grid=(n // tm,),
        in_specs=[row(512), row(512), const(w.shape), const((1, D_MODEL)), row(D_MODEL)],
        out_specs=row(D_MODEL),
        out_shape=jax.ShapeDtypeStruct((n, D_MODEL), F32),
        compiler_params=_params(("arbitrary",)),
        name="outproj",
    )(o1, o2, w, g, h)


def _bias_lookup(rb_ref, h, idx):
    def body(r, acc):
        return jnp.where(idx == r, rb_ref[h, r], acc)

    return lax.fori_loop(0, 2 * B_CLIP + 1, body, jnp.zeros(idx.shape, F32)) * LOG2E


def _bias_kernel(rb_ref, out_t_ref, out_s_ref, *, st):
    h = pl.program_id(0)
    far = B_BAND - B_CLIP
    end = rb_ref[h, 2 * B_CLIP] * LOG2E

    def near(shape, key_axis):
        j = lax.broadcasted_iota(jnp.int32, shape, key_axis) + far
        i = lax.broadcasted_iota(jnp.int32, shape, 1 - key_axis)
        return _bias_lookup(rb_ref, h, jnp.clip(B_BAND + i - j, -B_CLIP, B_CLIP) + B_CLIP)

    full_t = jnp.concatenate([jnp.full((far, TQ_B), end, F32), near((WIN_B - far, TQ_B), 0)], axis=0)
    cj = lax.shift_right_logical(lax.broadcasted_iota(jnp.int32, (WIN_B, TQ_B), 0), 6)
    ci = lax.shift_right_logical(lax.broadcasted_iota(jnp.int32, (WIN_B, TQ_B), 1), 6)
    out_t_ref[0] = jnp.where(jnp.logical_and(cj >= ci, cj <= ci + B_LEFT), full_t, NEG)
    out_s_ref[0] = jnp.concatenate([jnp.full((st, far), end, F32), near((st, WIN_B - far), 1)], axis=1)


def _bias_tiles(rel_bias, st):
    return pl.pallas_call(
        functools.partial(_bias_kernel, st=st),
        grid=(B_HEADS,),
        in_specs=[pl.BlockSpec(memory_space=pltpu.SMEM)],
        out_specs=[pl.BlockSpec((1, WIN_B, TQ_B), lambda h: (h // 2, 0, h % 2)),
                   pl.BlockSpec((1, st, WIN_B), lambda h: (h, 0, 0))],
        out_shape=[jax.ShapeDtypeStruct((B_HEADS // 2, WIN_B, 2 * TQ_B), F32),
                   jax.ShapeDtypeStruct((B_HEADS, st, WIN_B), F32)],
        compiler_params=_params(("arbitrary",)),
        name="bias_tile",
    )(rel_bias)


def _ones_rows(n):
    r = lax.broadcasted_iota(jnp.int32, (V_ROWS - LANES, n), 0)
    return jnp.where(r == 0, 1.0, 0.0).astype(BF16)


def _causal_attend_all(n_tiles, n_chains, tq, q_tile, k_rows, vt_cols, finish):
    half = tq // 2
    steps = []
    for qi in range(n_tiles):
        steps += [(qi, j * tq, tq, 0, False) for j in range(qi)]
        steps += [(qi, qi * tq, half, 0, True), (qi, qi * tq + half, half, half, True)]
    qcache = {}

    def scores(step):
        qi, k0, nk, q0, _ = step
        for c in range(n_chains):
            if (c, qi) not in qcache:
                qcache[(c, qi)] = q_tile(c, qi)
        return [_dot_nt(k_rows(c, k0, nk), qcache[(c, qi)][q0:tq, :]) for c in range(n_chains)]

    def chunk_mask(step):
        _, k0, nk, q0, _ = step
        j = lax.broadcasted_iota(jnp.int32, (nk, tq - q0), 0) + (k0 % tq)
        i = lax.broadcasted_iota(jnp.int32, (nk, tq - q0), 1) + q0
        return lax.shift_right_logical(j, 6) <= lax.shift_right_logical(i, 6)

    nxt = scores(steps[0])
    m, acc = [None] * n_chains, [None] * n_chains
    for idx, step in enumerate(steps):
        qi, k0, nk, q0, masked = step
        sts = nxt
        if idx + 1 < len(steps):
            nxt = scores(steps[idx + 1])
        if masked:
            msk = chunk_mask(step)
            sts = [jnp.where(msk, st, NEG) for st in sts]
        first = idx == 0 or steps[idx - 1][0] != qi
        pts, alphas = [], []
        for c in range(n_chains):
            top = jnp.max(sts[c], axis=0, keepdims=True)
            if not first:
                old = m[c][:, q0:tq]
                top = jnp.maximum(old, top)
                alphas.append(jnp.exp2(old - top))
                m[c] = top if q0 == 0 else jnp.concatenate([m[c][:, 0:q0], top], axis=1)
            else:
                m[c] = top
            pts.append(jnp.exp2(sts[c] - top).astype(BF16))
        vt = vt_cols(k0, nk)
        for c in range(n_chains):
            pv = _dot(vt, pts[c])
            if first:
                acc[c] = pv
            elif q0 == 0:
                acc[c] = alphas[c] * acc[c] + pv
            else:
                acc[c] = jnp.concatenate([acc[c][:, 0:q0], alphas[c] * acc[c][:, q0:tq] + pv], axis=1)
        if idx + 1 == len(steps) or steps[idx + 1][0] != qi:
            finish(qi, [a[0:LANES, :] * (1.0 / a[LANES:LANES + 1, :]) for a in acc])


def _lam(lq1, lk1, lq2, lk2, lam_init):
    return (jnp.exp(jnp.sum(lq1[...] * lk1[...], axis=1, keepdims=True))
            - jnp.exp(jnp.sum(lq2[...] * lk2[...], axis=1, keepdims=True)) + lam_init)


def _head_split(q):
    lo = _lane_lo()
    zero = jnp.zeros_like(q)
    return [jnp.where(lo, q, zero), jnp.where(lo, zero, q)]


def _attn_a_kernel(q_ref, k_ref, v_ref, lq1, lk1, lq2, lk2, gsub_ref, sg_ref, o_ref, kb_ref, vt_ref,
                   *, tq, t, lam_init):
    kb_ref[...] = k_ref[0].astype(BF16)
    vt_ref[0:LANES, :] = v_ref[0].T.astype(BF16)
    vt_ref[LANES:V_ROWS, :] = _ones_rows(t)
    lam = _lam(lq1, lk1, lq2, lk2, lam_init)
    tile = lambda i: slice(i * tq, (i + 1) * tq)
    split = {}

    def q_tile(c, qi):
        if qi not in split:
            split[qi] = _head_split(q_ref[0, tile(qi), :])
        return split[qi][c]

    def finish(qi, outs):
        o = outs[0].T - lam * outs[1].T
        o = _rms(o, gsub_ref[...]) * (1.0 - lam_init)
        o_ref[0, tile(qi), :] = (o * sg_ref[0, tile(qi), :].astype(F32)).astype(BF16)

    _causal_attend_all(t // tq, 2, tq, q_tile, lambda c, k0, n: kb_ref[k0:k0 + n, :],
                       lambda k0, n: vt_ref[:, k0:k0 + n], finish)


def _attn_a(q, k, v, lams, gsub, sg, lam_init):
    b, t, _ = q.shape
    tq = min(TQ, t)
    blk = pl.BlockSpec((1, t, LANES), lambda bi, h: (bi, 0, h))
    vec = lambda n: pl.BlockSpec((1, n), lambda bi, h: (0, 0))
    return pl.pallas_call(
        functools.partial(_attn_a_kernel, tq=tq, t=t, lam_init=lam_init),
        grid=(b, A_HEADS),
        in_specs=[blk, blk, blk, vec(A_QK), vec(A_QK), vec(A_QK), vec(A_QK), vec(A_V), blk],
        out_specs=blk,
        out_shape=jax.ShapeDtypeStruct((b, t, A_HEADS * A_V), BF16),
        scratch_shapes=[pltpu.VMEM((t, LANES), BF16), pltpu.VMEM((V_ROWS, t), BF16)],
        compiler_params=_params(("arbitrary", "arbitrary")),
        name="attn_a",
    )(q, k, v, *lams, gsub, sg)


def _attn_b_kernel(q_ref, k_ref, v_ref, bias_ref, sg_ref, o_ref, kb_ref, vt_ref, *, t):
    kb_ref[0:B_BAND, :] = jnp.zeros((B_BAND, LANES), BF16)
    kb_ref[B_BAND:B_BAND + t, :] = k_ref[0].astype(BF16)
    vt_ref[:, 0:B_BAND] = jnp.zeros((V_ROWS, B_BAND), BF16)
    vt_ref[0:LANES, B_BAND:B_BAND + t] = v_ref[0].T.astype(BF16)
    vt_ref[LANES:V_ROWS, B_BAND:B_BAND + t] = _ones_rows(t)
    key = lax.broadcasted_iota(jnp.int32, (WIN_B, 1), 0)
    row_lo = _row_lo()

    def tiles(q0s, first_valids):
        def scores(n):
            qa, qb = _head_split(q_ref[0, pl.ds(q0s[n], TQ_B), :])
            st = _dot_nt(kb_ref[pl.ds(q0s[n], WIN_B), :], jnp.concatenate([qa, qb], axis=0)) + bias_ref[0]
            if first_valids[n] > 0:
                st = jnp.where(key < first_valids[n], NEG, st)
            return st

        nxt = scores(0)
        for n in range(len(q0s)):
            st = nxt
            if n + 1 < len(q0s):
                nxt = scores(n + 1)
            pt = jnp.exp2(st - jnp.max(st, axis=0, keepdims=True)).astype(BF16)
            acc = _dot(vt_ref[:, pl.ds(q0s[n], WIN_B)], pt)
            inv = 1.0 / acc[LANES:LANES + 1, :]
            ot = jnp.where(row_lo, acc[0:LANES, 0:TQ_B] * inv[:, 0:TQ_B],
                           acc[0:LANES, TQ_B:2 * TQ_B] * inv[:, TQ_B:2 * TQ_B])
            rows = pl.ds(q0s[n], TQ_B)
            o_ref[0, rows, :] = (ot.T * sg_ref[0, rows, :].astype(F32)).astype(BF16)

    n_edge = min(B_BAND, t) // TQ_B
    tiles([qi * TQ_B for qi in range(n_edge)], [B_BAND - qi * TQ_B for qi in range(n_edge)])
    n_rest = (t - n_edge * TQ_B) // TQ_B
    per_iter = max(d for d in (12, 8, 4, 2, 1) if n_rest % d == 0)

    def body(i, carry):
        q0 = pl.multiple_of(n_edge * TQ_B + i * per_iter * TQ_B, per_iter * TQ_B)
        tiles([q0 + n * TQ_B for n in range(per_iter)], [0] * per_iter)
        return carry

    lax.fori_loop(0, n_rest // per_iter, body, 0)


def _attn_b(q, k, v, bias_t, sg):
    b, t, _ = q.shape
    assert t % TQ_B == 0
    blk = pl.BlockSpec((1, t, LANES), lambda bi, h: (bi, 0, h))
    sgblk = pl.BlockSpec((1, t, LANES), lambda bi, h: (bi, 0, h + A_HEADS))
    bblk = pl.BlockSpec((1, WIN_B, 2 * TQ_B), lambda bi, h: (h, 0, 0))
    return pl.pallas_call(
        functools.partial(_attn_b_kernel, t=t),
        grid=(b, B_HEADS // 2),
        in_specs=[blk, blk, blk, bblk, sgblk],
        out_specs=blk,
        out_shape=jax.ShapeDtypeStruct((b, t, B_HEADS * B_HD), BF16),
        scratch_shapes=[pltpu.VMEM((B_BAND + t, LANES), BF16), pltpu.VMEM((V_ROWS, B_BAND + t), BF16)],
        compiler_params=_params(("arbitrary", "arbitrary")),
        name="attn_b",
    )(q, k, v, bias_t, sg)


def _attn_c_kernel(q_ref, clat_ref, kr_ref, wuk_ref, wuv_ref, sg_ref, o_ref, kc_ref, vt_ref, *, tq, t):
    cl = clat_ref[0].astype(BF16)
    kr = kr_ref[0]
    for hh in range(2):
        kc_ref[hh] = (_dot(cl, wuk_ref[hh]) + kr).astype(BF16)
    vt_ref[0:LANES, :] = _dot(cl, wuv_ref[0]).T.astype(BF16)
    vt_ref[LANES:V_ROWS, :] = _ones_rows(t)
    tile = lambda i: slice(i * tq, (i + 1) * tq)
    row_lo = _row_lo()

    def finish(qi, outs):
        o = jnp.where(row_lo, outs[0], outs[1]).T
        o_ref[0, tile(qi), :] = (o * sg_ref[0, tile(qi), :].astype(F32)).astype(BF16)

    _causal_attend_all(t // tq, 2, tq, lambda c, qi: q_ref[0, tile(qi), c * LANES:(c + 1) * LANES],
                       lambda c, k0, n: kc_ref[c, k0:k0 + n, :], lambda k0, n: vt_ref[:, k0:k0 + n], finish)


def _attn_c(qc, clat, kr, wuk, wuv, sg):
    b, t, _ = qc.shape
    tq = min(TQ, t)
    return pl.pallas_call(
        functools.partial(_attn_c_kernel, tq=tq, t=t),
        grid=(b, C_HEADS // 2),
        in_specs=[pl.BlockSpec((1, t, 2 * LANES), lambda bi, h: (bi, 0, h)),
                  pl.BlockSpec((1, t, C_KVL), lambda bi, h: (bi, 0, 0)),
                  pl.BlockSpec((1, t, LANES), lambda bi, h: (bi, 0, 0)),
                  pl.BlockSpec((2, C_KVL, LANES), lambda bi, h: (h, 0, 0)),
                  pl.BlockSpec((1, C_KVL, LANES), lambda bi, h: (h, 0, 0)),
                  pl.BlockSpec((1, t, LANES), lambda bi, h: (bi, 0, h))],
        out_specs=pl.BlockSpec((1, t, LANES), lambda bi, h: (bi, 0, h)),
        out_shape=jax.ShapeDtypeStruct((b, t, C_HEADS * C_V), BF16),
        scratch_shapes=[pltpu.VMEM((2, t, LANES), BF16), pltpu.VMEM((V_ROWS, t), BF16)],
        compiler_params=_params(("arbitrary", "arbitrary")),
        name="attn_c",
    )(qc, clat, kr, wuk, wuv, sg)


def _suffix_sums_t(msp, negut, tk):
    mb = msp.astype(BF16)
    if msp.shape[0] == tk:
        return _dot(negut, mb)
    late = _dot(negut, mb[tk:2 * tk, :])
    return jnp.concatenate([_dot(negut, mb[0:tk, :]) + late[0:1, :], late], axis=0)


def _stick_units_t(qs, ks, vts, negut, runs, tri_t, tk):
    n = len(qs)

    def diag_only(x):
        return x if tri_t is None else jnp.concatenate([x[0:tk, :], jnp.where(tri_t, x[tk:2 * tk, :], 0.0)], axis=0)

    zts, stage1, out = [None] * n, [None] * n, [None] * n
    for step in range(n + 2):
        if step < n:
            zts[step] = _dot_nt(ks[step], qs[step])
        c = step - 1
        if 0 <= c < n:
            later = _suffix_sums_t(diag_only(_softplus(zts[c])), negut, tk)
            stage1[c] = (zts[c] + later, later[0:1, :])
        c = step - 2
        if 0 <= c < n:
            x, total = stage1[c]
            if runs is not None:
                x = x + runs[c]
            out[c] = (total, _dot(vts[c], diag_only(jnp.exp(x)).astype(BF16)))
    return out


def _attn_d_kernel(q_ref, k_ref, v_ref, negut_ref, sg_ref, o_ref, kb_ref, vt_ref, *, tk, t, nq):
    qt = pl.program_id(2)

    @pl.when(qt == 0)
    def _():
        kb_ref[0:tk, :] = jnp.zeros((tk, LANES), BF16)
        vt_ref[:, 0:tk] = jnp.zeros((LANES, tk), BF16)
        kb_ref[tk:tk + t, :] = k_ref[0].astype(BF16)
        vt_ref[:, tk:tk + t] = v_ref[0].T.astype(BF16)

    qs, tile_of = [], []
    for h in range(nq):
        qs += _head_split(q_ref[0, h * tk:(h + 1) * tk, :])
        tile_of += [h, h]
    j = lax.broadcasted_iota(jnp.int32, (tk, tk), 0)
    i = lax.broadcasted_iota(jnp.int32, (tk, tk), 1)
    tri_t = j < i
    wins = [pl.ds(pl.multiple_of((nq * qt + h) * tk, tk), 2 * tk) for h in tile_of]
    first = _stick_units_t(qs, [kb_ref[w, :] for w in wins], [vt_ref[:, w] for w in wins], negut_ref[...],
                           None, tri_t, tk)
    runs = [f[0] for f in first]
    accs = [f[1] for f in first]

    def top(rs):
        return functools.reduce(jnp.maximum, [jnp.max(r) for r in rs])

    def cond(carry):
        return jnp.logical_and(carry[0] < nq * qt + nq - 2, carry[1] > -104.0)

    def body(carry):
        n, _, runs, accs = carry
        tiles = [pl.ds(pl.multiple_of(jnp.maximum(nq * qt + h - 1 - n, 0) * tk, tk), tk) for h in tile_of]
        res = _stick_units_t(qs, [kb_ref[w, :] for w in tiles], [vt_ref[:, w] for w in tiles],
                             negut_ref[...], runs, None, tk)
        runs = [r + d for r, (d, _) in zip(runs, res)]
        accs = [a + c for a, (_, c) in zip(accs, res)]
        return n + 1, top(runs), runs, accs

    _, _, _, accs = lax.while_loop(cond, body, (jnp.int32(0), top(runs), runs, accs))
    row_lo = _row_lo()
    for h in range(nq):
        ot = jnp.where(row_lo, accs[2 * h], accs[2 * h + 1])
        rows = slice(h * tk, (h + 1) * tk)
        o_ref[0, rows, :] = (ot.T * sg_ref[0, rows, :].astype(F32)).astype(BF16)


def _attn_d(q, k, v, negut, sg):
    b, t, _ = q.shape
    tk = min(TQ_D, t // NQ_D)
    qblk = pl.BlockSpec((1, NQ_D * tk, LANES), lambda bi, h, qi: (bi, qi, h))
    sgblk = pl.BlockSpec((1, NQ_D * tk, LANES), lambda bi, h, qi: (bi, qi, h + C_HEADS // 2))
    kvblk = pl.BlockSpec((1, t, LANES), lambda bi, h, qi: (bi, 0, h))
    ublk = pl.BlockSpec((tk, tk), lambda bi, h, qi: (0, 0))
    return pl.pallas_call(
        functools.partial(_attn_d_kernel, tk=tk, t=t, nq=NQ_D),
        grid=(b, D_HEADS // 2, t // (NQ_D * tk)),
        in_specs=[qblk, kvblk, kvblk, ublk, sgblk],
        out_specs=qblk,
        out_shape=jax.ShapeDtypeStruct((b, t, D_HEADS * D_HD), BF16),
        scratch_shapes=[pltpu.VMEM((tk + t, LANES), BF16), pltpu.VMEM((LANES, tk + t), BF16)],
        compiler_params=_params(("arbitrary", "arbitrary", "arbitrary")),
        name="attn_d",
    )(q, k, v, negut, sg)


def _two_part_softmax(q, kct, kn, vc, vn, bc=None, bn=None, vc_t=False):
    sc = _dot(q, kct)
    sn = _dot_nt(q, kn)
    if bc is not None:
        sc = sc + bc
        sn = sn + bn
    m = jnp.maximum(jnp.max(sc, axis=1, keepdims=True), jnp.max(sn, axis=1, keepdims=True))
    pc = jnp.exp2(sc - m)
    pn = jnp.exp2(sn - m)
    l = jnp.sum(pc, axis=1, keepdims=True) + jnp.sum(pn, axis=1, keepdims=True)
    pvc = _dot_nt(pc.astype(BF16), vc) if vc_t else _dot(pc.astype(BF16), vc)
    return (pvc + _dot(pn.astype(BF16), vn)) * (1.0 / l)


def _samp_a_kernel(q_ref, ck_ref, cv_ref, nk_ref, nv_ref, lq1, lk1, lq2, lk2, gsub_ref, sg_ref, o_ref,
                   *, lam_init):
    lam = _lam(lq1, lk1, lq2, lk2, lam_init)
    for h in range(A_HEADS):
        sl = slice(h * LANES, (h + 1) * LANES)
        kct, vc = ck_ref[0, h].astype(BF16), cv_ref[0, :, h, :].astype(BF16)
        kn, vn = nk_ref[0, :, sl].astype(BF16), nv_ref[0, :, sl].astype(BF16)
        q0, q1 = _head_split(q_ref[0, :, sl])
        o0 = _two_part_softmax(q0, kct, kn, vc, vn)
        o1 = _two_part_softmax(q1, kct, kn, vc, vn)
        o = _rms(o0 - lam * o1, gsub_ref[...]) * (1.0 - lam_init)
        o_ref[0, :, sl] = (o * sg_ref[0, :, sl].astype(F32)).astype(BF16)


def _samp_b_kernel(q_ref, ck_ref, cv_ref, nk_ref, nv_ref, bias_ref, sg_ref, o_ref, *, band, st):
    lo = _lane_lo()
    for hp in range(B_HEADS // 2):
        sl = slice(hp * LANES, (hp + 1) * LANES)
        kct, vct = ck_ref[0, hp].astype(BF16), cv_ref[0, hp].astype(BF16)
        kn, vn = nk_ref[0, :, sl].astype(BF16), nv_ref[0, :, sl].astype(BF16)
        outs = []
        for hh, qm in enumerate(_head_split(q_ref[0, :, sl])):
            bias = bias_ref[2 * hp + hh]
            outs.append(_two_part_softmax(qm, kct, kn, vct, vn, bias[:, 0:band], bias[:, band:band + st],
                                          vc_t=True))
        o = jnp.where(lo, outs[0], outs[1])
        o_ref[0, :, sl] = (o * sg_ref[0, :, sl].astype(F32)).astype(BF16)


def _samp_c_kernel(q_ref, clat_ref, krp_ref, nclat_ref, nkr_ref, wukt_ref, wuv_ref, sg_ref, o_ref, *, st):
    lo = _lane_lo()
    cl = clat_ref[0].astype(BF16)
    ncl = nclat_ref[0].astype(BF16)
    past = krp_ref.shape[2]
    krct = jnp.concatenate([jnp.zeros((C_NOPE, past), BF16), krp_ref[0].astype(BF16),
                            jnp.zeros((LANES - C_NOPE - C_ROPE, past), BF16)], axis=0)
    krn = nkr_ref[0].astype(BF16)
    qh = [q_ref[0, :, h * LANES:(h + 1) * LANES] for h in range(C_HEADS)]
    q_all = jnp.concatenate(qh, axis=0)
    q_lat = jnp.concatenate([_dot(qh[h], wukt_ref[h]) for h in range(C_HEADS)], axis=0).astype(BF16)
    sc = _dot_nt(q_lat, cl) + _dot(q_all, krct)
    sn = _dot_nt(q_lat, ncl) + _dot_nt(q_all, krn)
    m = jnp.maximum(jnp.max(sc, axis=1, keepdims=True), jnp.max(sn, axis=1, keepdims=True))
    pc = jnp.exp2(sc - m)
    pn = jnp.exp2(sn - m)
    l = jnp.sum(pc, axis=1, keepdims=True) + jnp.sum(pn, axis=1, keepdims=True)
    o_lat = ((_dot(pc.astype(BF16), cl) + _dot(pn.astype(BF16), ncl)) * (1.0 / l)).astype(BF16)
    for hp in range(C_HEADS // 2):
        sl = slice(hp * LANES, (hp + 1) * LANES)
        oa = _dot(o_lat[(2 * hp) * st:(2 * hp + 1) * st, :], wuv_ref[hp])
        ob = _dot(o_lat[(2 * hp + 1) * st:(2 * hp + 2) * st, :], wuv_ref[hp])
        o_ref[0, :, sl] = (jnp.where(lo, oa, ob) * sg_ref[0, :, sl].astype(F32)).astype(BF16)


def _stick_unit(q, k, v, negu, run, tri, kv_t=False):
    z = _dot(q, k) if kv_t else _dot_nt(q, k)
    sp = _softplus(z)
    msp = sp if tri is None else jnp.where(tri, sp, 0.0)
    later = _dot(msp.astype(BF16), negu)
    x = z - sp + later
    if run is not None:
        x = x + run
    a = jnp.exp(x)
    if tri is not None:
        a = jnp.where(tri, a, 0.0)
    total = later[:, 0:1] - msp[:, 0:1]
    return total, (_dot_nt(a.astype(BF16), v) if kv_t else _dot(a.astype(BF16), v))


def _neg_suffix(n):
    r = lax.broadcasted_iota(jnp.int32, (n, n), 0)
    c = lax.broadcasted_iota(jnp.int32, (n, n), 1)
    return jnp.where(r > c, -1.0, 0.0).astype(BF16)


def _samp_d_kernel(q_ref, ck_ref, cv_ref, nk_ref, nv_ref, sg_ref, o_ref, *, past, st, tk):
    lo = _lane_lo()
    negu_new = _neg_suffix(st)
    r = lax.broadcasted_iota(jnp.int32, (st, st), 0)
    c = lax.broadcasted_iota(jnp.int32, (st, st), 1)
    tri = c < r
    negu = _neg_suffix(tk)
    n_chunks = past // tk
    for hp in range(D_HEADS // 2):
        sl = slice(hp * LANES, (hp + 1) * LANES)
        kn, vn = nk_ref[0, :, sl].astype(BF16), nv_ref[0, :, sl].astype(BF16)
        qa, qb = _head_split(q_ref[0, :, sl])
        run_a, acc_a = _stick_unit(qa, kn, vn, negu_new, None, tri)
        run_b, acc_b = _stick_unit(qb, kn, vn, negu_new, None, tri)

        def cond(carry):
            return jnp.logical_and(carry[0] < n_chunks, carry[1] > -104.0)

        def body(carry, qa=qa, qb=qb, hp=hp):
            n, _, run_a, acc_a, run_b, acc_b = carry
            keys = pl.ds(pl.multiple_of((n_chunks - 1 - n) * tk, tk), tk)
            kct, vct = ck_ref[0, hp, :, keys].astype(BF16), cv_ref[0, hp, :, keys].astype(BF16)
            da, ca = _stick_unit(qa, kct, vct, negu, run_a, None, kv_t=True)
            db, cb = _stick_unit(qb, kct, vct, negu, run_b, None, kv_t=True)
            run_a, run_b = run_a + da, run_b + db
            return n + 1, jnp.maximum(jnp.max(run_a), jnp.max(run_b)), run_a, acc_a + ca, run_b, acc_b + cb

        init = (jnp.int32(0), jnp.maximum(jnp.max(run_a), jnp.max(run_b)), run_a, acc_a, run_b, acc_b)
        _, _, _, acc_a, _, acc_b = lax.while_loop(cond, body, init)
        o = jnp.where(lo, acc_a, acc_b)
        o_ref[0, :, sl] = (o * sg_ref[0, :, sl].astype(F32)).astype(BF16)


def _samp_call(body, name, b, st, ins, specs):
    return pl.pallas_call(
        body,
        grid=(b,),
        in_specs=specs,
        out_specs=pl.BlockSpec((1, st, 512), lambda bi: (bi, 0, 0)),
        out_shape=jax.ShapeDtypeStruct((b, st, 512), BF16),
        compiler_params=_params(("arbitrary",)),
        name=name,
    )(*ins)


def _per_batch(shape, col=0):
    return pl.BlockSpec((1,) + tuple(shape[1:]), lambda bi: (bi,) + (0,) * (len(shape) - 2) + (col,))


def _whole(shape):
    return pl.BlockSpec(tuple(shape), lambda bi: (0,) * len(shape))


def _rope_tables(pos, rot, period, offset):
    half = rot // 2
    inv_freq = ROPE_THETA ** (-jnp.arange(half, dtype=F32) / half)
    ang = pos.astype(F32)[:, None] * inv_freq[None, :]
    cos, sin = jnp.cos(ang), jnp.sin(ang)
    n = pos.shape[0]
    c = jnp.ones((n, period), F32).at[:, offset:offset + half].set(cos).at[:, offset + half:offset + rot].set(cos)
    s1 = jnp.zeros((n, period), F32).at[:, offset:offset + half].set(-sin)
    s2 = jnp.zeros((n, period), F32).at[:, offset + half:offset + rot].set(sin)
    reps = LANES // period
    return tuple(jnp.tile(x, (1, reps)) for x in (c, s1, s2))


def kernel(x_prompt, x_sample, cache_a_k, cache_a_v, cache_b_k, cache_b_v, cache_c_latent, cache_c_krope,
           cache_d_k, cache_d_v, g_pre0, w_in0, lam_q1, lam_k1, lam_q2, lam_k2, g_sub_a, rel_bias_b, w_out0,
           g_post0, g_pre1, w_in1, g_cq, w_uq, g_ckv, w_uk, w_uv, w_out1, g_post1):
    b, t, _ = x_prompt.shape
    sb, st, _ = x_sample.shape
    past = cache_a_k.shape[1]
    band = cache_b_k.shape[1]
    assert band == B_BAND and t % TQ == 0 and past % TQ_D == 0
    ns = sb * st
    row = lambda x: x.reshape(1, -1)

    w0 = w_in0.astype(BF16)
    wo0 = w_out0.astype(BF16)
    wo1 = w_out1.astype(BF16)
    w1 = jnp.concatenate([w_in1[:, 0:1024], w_in1[:, 1056:3616], jnp.zeros((D_MODEL, 64), F32),
                          w_in1[:, 1024:1056], jnp.zeros((D_MODEL, 32), F32)], axis=1).astype(BF16)
    wuq = jnp.pad(w_uq.reshape(C_QL, C_HEADS, C_NOPE + C_ROPE), ((0, 0), (0, 0), (0, 32)))
    wuq = wuq.reshape(C_QL, C_HEADS * LANES).astype(BF16)
    wuk = jnp.pad(w_uk.transpose(1, 0, 2), ((0, 0), (0, 0), (0, LANES - C_NOPE))).astype(BF16)
    wuv = w_uv.reshape(C_KVL, C_HEADS // 2, 2 * C_V).transpose(1, 0, 2).astype(BF16)
    n_u = min(TQ_D, t // NQ_D)
    negut = -jnp.triu(jnp.ones((n_u, n_u), BF16))
    lams = tuple(row(x) for x in (lam_q1, lam_k1, lam_q2, lam_k2))
    lam_init = 0.8 - 0.6 * math.exp(-0.3 * 0)

    pos_p = jnp.arange(t)
    pos_s = past + jnp.arange(st)
    tabs0_p = _rope_tables(pos_p, A_ROT, A_QK, 0)
    tabs0_s = tuple(jnp.tile(x, (sb, 1)) for x in _rope_tables(pos_s, A_ROT, A_QK, 0))
    tabs1_p = _rope_tables(pos_p, C_ROPE, LANES, C_NOPE)
    tabs1_s = tuple(jnp.tile(x, (sb, 1)) for x in _rope_tables(pos_s, C_ROPE, LANES, C_NOPE))

    bias_t, bias_s = _bias_tiles(rel_bias_b, st)

    aq, ak, av, bq, bk, bv, sg0, avb = _proj0(x_prompt, row(g_pre0), w0, tabs0_p, 512)
    oa = _attn_a(aq, ak, avb, lams, row(g_sub_a), sg0, lam_init)
    ob = _attn_b(bq, bk, bv, bias_t, sg0)
    h1 = _outproj(oa.reshape(b * t, 512), ob.reshape(b * t, 512), wo0, row(g_post0),
                  x_prompt.reshape(b * t, D_MODEL), 1024)

    def keys_minor(x):
        n = x.shape[1]
        return jnp.moveaxis(x, 1, -1).reshape(sb, -1, LANES, n)

    xs = x_sample.reshape(1, ns, D_MODEL)
    s_out = _proj0(xs, row(g_pre0), w0, tabs0_s, ns)
    aq_s, ak_s, av_s, bq_s, bk_s, bv_s, sg0_s, _ = (x.reshape(sb, st, -1) for x in s_out)
    vec = lambda n: _whole((1, n))
    oa_s = _samp_call(
        functools.partial(_samp_a_kernel, lam_init=lam_init), "samp_a", sb, st,
        (aq_s, keys_minor(cache_a_k), cache_a_v, ak_s, av_s,
         *lams, row(g_sub_a), sg0_s),
        [_per_batch((sb, st, 512)), _per_batch((sb, A_HEADS, LANES, past)), _per_batch((sb, past, A_HEADS, A_V)),
         _per_batch((sb, st, 512)), _per_batch((sb, st, 512)), vec(A_QK), vec(A_QK), vec(A_QK), vec(A_QK),
         vec(A_V), _per_batch((sb, st, 512))])
    ob_s = _samp_call(
        functools.partial(_samp_b_kernel, band=band, st=st), "samp_b", sb, st,
        (bq_s, keys_minor(cache_b_k), keys_minor(cache_b_v), bk_s, bv_s, bias_s, sg0_s),
        [_per_batch((sb, st, 512)), _per_batch((sb, B_HEADS // 2, LANES, band)),
         _per_batch((sb, B_HEADS // 2, LANES, band)),
         _per_batch((sb, st, 512)), _per_batch((sb, st, 512)),
         pl.BlockSpec((B_HEADS, st, WIN_B), lambda bi: (0, 0, 0)), _per_batch((sb, st, 512), col=1)])
    hs1 = _outproj(oa_s.reshape(ns, 512), ob_s.reshape(ns, 512), wo0, row(g_post0),
                   x_sample.reshape(ns, D_MODEL), ns)

    qc, clat, kr, dq, dk, dv, sg1 = _proj1(h1.reshape(b, t, D_MODEL), row(g_pre1), w1, row(g_cq), wuq,
                                           row(g_ckv), tabs1_p, 512)
    oc = _attn_c(qc, clat, kr, wuk, wuv, sg1)
    od = _attn_d(dq, dk, dv, negut, sg1)
    h2 = _outproj(oc.reshape(b * t, 512), od.reshape(b * t, 512), wo1, row(g_post1), h1, 1024)

    s_out = _proj1(hs1.reshape(1, ns, D_MODEL), row(g_pre1), w1, row(g_cq), wuq, row(g_ckv), tabs1_s, ns)
    qc_s, clat_s, kr_s, dq_s, dk_s, dv_s, sg1_s = (x.reshape(sb, st, -1) for x in s_out)
    krt = jnp.moveaxis(cache_c_krope, 1, -1)
    wukt = wuk.transpose(0, 2, 1)
    oc_s = _samp_call(
        functools.partial(_samp_c_kernel, st=st), "samp_c", sb, st,
        (qc_s, cache_c_latent, krt, clat_s, kr_s, wukt, wuv, sg1_s),
        [_per_batch((sb, st, 1024)), _per_batch((sb, past, C_KVL)), _per_batch((sb, C_ROPE, past)),
         _per_batch((sb, st, C_KVL)), _per_batch((sb, st, LANES)), _whole(wukt.shape), _whole(wuv.shape),
         _per_batch((sb, st, 512))])
    od_s = _samp_call(
        functools.partial(_samp_d_kernel, past=past, st=st, tk=TQ_D), "samp_d", sb, st,
        (dq_s, keys_minor(cache_d_k), keys_minor(cache_d_v), dk_s, dv_s, sg1_s),
        [_per_batch((sb, st, 512)), _per_batch((sb, D_HEADS // 2, LANES, past)),
         _per_batch((sb, D_HEADS // 2, LANES, past)),
         _per_batch((sb, st, 512)), _per_batch((sb, st, 512)), _per_batch((sb, st, 512), col=1)])
    hs2 = _outproj(oc_s.reshape(ns, 512), od_s.reshape(ns, 512), wo1, row(g_post1), hs1, ns)

    b_rows = min(B_BAND, t)
    return (h2.reshape(b, t, D_MODEL), hs2.reshape(sb, st, D_MODEL),
            ak.reshape(b, t, A_HEADS, 2, A_QK), av,
            bk[:, t - b_rows:].reshape(b, b_rows, B_HEADS, B_HD), bv[:, t - b_rows:].reshape(b, b_rows, B_HEADS, B_HD),
            clat, kr[:, :, C_NOPE:C_NOPE + C_ROPE],
            dk.reshape(b, t, D_HEADS, D_HD), dv.reshape(b, t, D_HEADS, D_HD),
            ak_s.reshape(sb, st, A_HEADS, 2, A_QK), av_s.reshape(sb, st, A_HEADS, A_V),
            bk_s.reshape(sb, st, B_HEADS, B_HD), bv_s.reshape(sb, st, B_HEADS, B_HD),
            clat_s, kr_s[:, :, C_NOPE:C_NOPE + C_ROPE],
            dk_s.reshape(sb, st, D_HEADS, D_HD), dv_s.reshape(sb, st, D_HEADS, D_HD))
```
